```python
import math, functools
import jax, jax.numpy as jnp
from jax import lax
import numpy as np

D_MODEL = 2048
BATCH = 1
SEQ = 8192
DEPTH = 1
DEC_BATCH = 128
DEC_SEQ = 4
PAST_LEN = 2048
PAGE_SIZE = 128

HEAD_DIM = 128
N_FOX = D_MODEL // (2 * HEAD_DIM)
N_DSA = D_MODEL // (2 * HEAD_DIM)
FOX_W = N_FOX * HEAD_DIM
DSA_W = N_DSA * HEAD_DIM
IDX_HEADS = 16
IDX_DIM = 64
TOPK_MAX = 256
QBLK = 128
D_FF = ((8 * D_MODEL // 3 + 255) // 256) * 256
PLE_DIM = 256
NUM_BUCKETS = 32
MAX_DISTANCE = 128
FORGET_BIAS_INIT = 3.0
RMS_EPS = 1e-6
ATT_SCALE = HEAD_DIM ** -0.5
IDX_SCALE = IDX_DIM ** -0.5
IDX_W_SCALE = IDX_HEADS ** -0.5
N_IN = 3 * FOX_W + N_FOX + 3 * DSA_W + IDX_HEADS * IDX_DIM + IDX_DIM + IDX_HEADS + 2 * D_MODEL

kernel_name = 'fox_dsa_gated_macaron_decode_step'


def _rms(x, g):
    xf = x.astype(jnp.float32)
    y = xf * lax.rsqrt(jnp.mean(xf * xf, axis=-1, keepdims=True) + RMS_EPS)
    return (y * g.astype(jnp.float32)).astype(x.dtype)


def _swiglu(h, w1, w3, w2):
    return (jax.nn.silu(h @ w1) * (h @ w3)) @ w2


def _split(z):
    sizes = [FOX_W, FOX_W, FOX_W, N_FOX, DSA_W, DSA_W, DSA_W,
             IDX_HEADS * IDX_DIM, IDX_DIM, IDX_HEADS, 2 * D_MODEL]
    offs = np.cumsum(sizes)[:-1].tolist()
    return jnp.split(z, offs, axis=-1)


def _blocks(t):
    b, s = t.shape[:2]
    return jnp.swapaxes(t.reshape(b, s // QBLK, QBLK, *t.shape[2:]), 0, 1)


def _unblocks(t):
    nb, b, q = t.shape[:3]
    return jnp.swapaxes(t, 0, 1).reshape(b, nb * q, *t.shape[3:])


def _gather_pages(pool, page_table):
    g = pool[page_table]
    b, npg, pg = g.shape[:3]
    return g.reshape(b, npg * pg, *g.shape[3:])


def _take_rows(rows, idx):
    return jax.vmap(lambda r, i: r[i])(rows, idx)


def _t5_bucket(dist):
    n = jnp.maximum(dist, 0)
    exact = NUM_BUCKETS // 2
    nf = jnp.maximum(n, 1).astype(jnp.float32)
    large = exact + (jnp.log(nf / exact) / math.log(MAX_DISTANCE / exact)
                     * (NUM_BUCKETS - exact)).astype(jnp.int32)
    return jnp.where(n < exact, n, jnp.minimum(large, NUM_BUCKETS - 1))


def _index_scores(q_i, w_i, k_i):
    sc = jnp.einsum('bthd,bsd->bths', q_i, k_i, preferred_element_type=jnp.float32) * IDX_SCALE
    return jnp.einsum('bths,bth->bts', jax.nn.relu(sc), w_i.astype(jnp.float32) * IDX_W_SCALE)


def _sparse_attend(q, k_sel, v_sel, idx, qpos, rel_bias):
    dist = qpos[None, :, None] - idx
    s = jnp.einsum('bthd,btkhd->bhtk', q, k_sel, preferred_element_type=jnp.float32) * ATT_SCALE
    bias = jnp.moveaxis(rel_bias[_t5_bucket(dist)].astype(jnp.float32), -1, 1)
    s = jnp.where((dist >= 0)[:, None], s + bias, -jnp.inf)
    p = jax.nn.softmax(s, axis=-1)
    return jnp.einsum('bhtk,btkhd->bthd', p.astype(v_sel.dtype), v_sel)


def _fox_prompt(q, k, v, logf):
    s_len = q.shape[1]
    f_cum = jnp.cumsum(logf, axis=1)
    f_t = jnp.swapaxes(f_cum, 1, 2)
    kpos = jnp.arange(s_len)

    def blk(xs):
        i, qb, fb = xs
        qpos = i * QBLK + jnp.arange(QBLK)
        s = jnp.einsum('bqhd,bshd->bhqs', qb, k, preferred_element_type=jnp.float32) * ATT_SCALE
        s = s + jnp.swapaxes(fb, 1, 2)[..., None] - f_t[:, :, None, :]
        s = jnp.where(kpos[None, :] <= qpos[:, None], s, -jnp.inf)
        p = jax.nn.softmax(s, axis=-1)
        return jnp.einsum('bhqs,bshd->bqhd', p.astype(v.dtype), v)

    out = lax.map(blk, (jnp.arange(s_len // QBLK), _blocks(q), _blocks(f_cum)))
    return _unblocks(out)


def _dsa_prompt(q, k, v, q_i, k_i, w_i, rel_bias):
    s_len = q.shape[1]
    n_sel = min(TOPK_MAX, s_len // 4)
    kpos = jnp.arange(s_len)

    def blk(xs):
        i, qb, qib, wib = xs
        qpos = i * QBLK + jnp.arange(QBLK)
        scores = _index_scores(qib, wib, k_i)
        scores = jnp.where(kpos[None, None, :] <= qpos[None, :, None], scores, -jnp.inf)
        _, idx = lax.top_k(scores, n_sel)
        return _sparse_attend(qb, _take_rows(k, idx), _take_rows(v, idx), idx, qpos, rel_bias)

    out = lax.map(blk, (jnp.arange(s_len // QBLK), _blocks(q), _blocks(q_i), _blocks(w_i)))
    return _unblocks(out)


def _fox_sample(q, k_new, v_new, lf_new, pool_k, pool_v, pool_lf, page_table):
    t_len = q.shape[1]
    k_all = jnp.concatenate([_gather_pages(pool_k, page_table), k_new], axis=1)
    v_all = jnp.concatenate([_gather_pages(pool_v, page_table), v_new], axis=1)
    lf_all = jnp.concatenate([_gather_pages(pool_lf, page_table).astype(jnp.float32), lf_new], axis=1)
    past = k_all.shape[1] - t_len
    f_t = jnp.swapaxes(jnp.cumsum(lf_all, axis=1), 1, 2)
    qpos = past + jnp.arange(t_len)
    kpos = jnp.arange(past + t_len)
    s = jnp.einsum('bqhd,bshd->bhqs', q, k_all, preferred_element_type=jnp.float32) * ATT_SCALE
    s = s + f_t[:, :, past:, None] - f_t[:, :, None, :]
    s = jnp.where(kpos[None, :] <= qpos[:, None], s, -jnp.inf)
    p = jax.nn.softmax(s, axis=-1)
    return jnp.einsum('bhqs,bshd->bqhd', p.astype(v_all.dtype), v_all)


def _dsa_sample(q, k_new, v_new, q_i, ki_new, w_i, pool_k, pool_v, pool_ik, page_table, rel_bias):
    b, t_len = q.shape[:2]
    ki_all = jnp.concatenate([_gather_pages(pool_ik, page_table), ki_new], axis=1)
    l_len = ki_all.shape[1]
    past = l_len - t_len
    n_sel = min(TOPK_MAX, l_len // 4)
    qpos = past + jnp.arange(t_len)
    scores = _index_scores(q_i, w_i, ki_all)
    scores = jnp.where(jnp.arange(l_len)[None, None, :] <= qpos[None, :, None], scores, -jnp.inf)
    _, idx = lax.top_k(scores, n_sel)
    page = pool_k.shape[1]
    pidx = jnp.minimum(idx, past - 1)
    phys = jnp.take_along_axis(page_table, (pidx // page).reshape(b, -1), axis=1).reshape(idx.shape)
    off = pidx % page
    nidx = jnp.clip(idx - past, 0, t_len - 1)
    in_past = (idx < past)[..., None, None]
    k_sel = jnp.where(in_past, pool_k[phys, off], _take_rows(k_new, nidx))
    v_sel = jnp.where(in_past, pool_v[phys, off], _take_rows(v_new, nidx))
    return _sparse_attend(q, k_sel, v_sel, idx, qpos, rel_bias)


def _mix_prompt(q_a, k_a, v_a, lf_a, q_b, k_b, v_b, q_i, k_i, w_i, rel_bias):
    return (_fox_prompt(q_a, k_a, v_a, lf_a),
            _dsa_prompt(q_b, k_b, v_b, q_i, k_i, w_i, rel_bias))


def _mix_sample(q_a, k_a, v_a, lf_a, q_b, k_b, v_b, q_i, k_i, w_i, rel_bias, page_table,
                fox_k, fox_v, fox_lf, dsa_k, dsa_v, idx_k):
    return (_fox_sample(q_a, k_a, v_a, lf_a, fox_k, fox_v, fox_lf, page_table),
            _dsa_sample(q_b, k_b, v_b, q_i, k_i, w_i, dsa_k, dsa_v, idx_k, page_table, rel_bias))


def _block(x, p, mix_fn, g_ffn1, w1_pre, w3_pre, w2_pre, g_mix, w_in, b_forget,
           w_branch_fox, w_branch_dsa, w_out, g_ffn2, w1_post, w3_post, w2_post,
           g_ple, w_ple, w_ple_gate):
    b, t_len, _ = x.shape
    x = x + 0.5 * _swiglu(_rms(x, g_ffn1), w1_pre, w3_pre, w2_pre)
    h = _rms(x, g_mix)
    q_a, k_a, v_a, f_a, q_b, k_b, v_b, q_i, k_i, w_i, gate = _split(h @ w_in)
    heads = lambda t: t.reshape(b, t_len, -1, HEAD_DIM)
    q_a, k_a, v_a, q_b, k_b, v_b = [heads(t) for t in (q_a, k_a, v_a, q_b, k_b, v_b)]
    q_i = q_i.reshape(b, t_len, IDX_HEADS, IDX_DIM)
    lf_a = jax.nn.log_sigmoid((f_a + b_forget).astype(jnp.float32))
    o_a, o_b = mix_fn(q_a, k_a, v_a, lf_a, q_b, k_b, v_b, q_i, k_i, w_i)
    u_a = (o_a.reshape(b, t_len, FOX_W) @ w_branch_fox).astype(jnp.float32)
    u_b = (o_b.reshape(b, t_len, DSA_W) @ w_branch_dsa).astype(jnp.float32)
    g_a, g_b = jnp.split(jax.nn.sigmoid(gate.astype(jnp.float32)), 2, axis=-1)
    x = x + (g_a * u_a + g_b * u_b).astype(x.dtype) @ w_out
    x = x + 0.5 * _swiglu(_rms(x, g_ffn2), w1_post, w3_post, w2_post)
    x = x + (p @ w_ple) * jax.nn.sigmoid(_rms(x, g_ple) @ w_ple_gate)
    return x, (k_a, v_a, lf_a, k_b, v_b, k_i)


def setup_inputs(seed: int = 0) -> dict:
    key = jax.random.key(seed)
    ks = iter(jax.random.split(key, 48))
    f32 = jnp.float32
    nrm = lambda shape, scale: jax.random.normal(next(ks), shape, f32) * scale
    gain = lambda shape: 1.0 + 0.05 * jax.random.normal(next(ks), shape, f32)
    n_pages = PAST_LEN // PAGE_SIZE
    n_used = DEC_BATCH * n_pages
    n_pool = n_used + max(1, n_used // 4)
    page_table = jax.random.permutation(next(ks), n_pool)[:n_used].reshape(DEC_BATCH, n_pages).astype(jnp.int32)
    d = D_MODEL
    return {
        'x_prompt': nrm((BATCH, SEQ, d), 1.0),
        'x_sample': nrm((DEC_BATCH, DEC_SEQ, d), 1.0),
        'cache_fox_k': nrm((DEPTH, n_pool, PAGE_SIZE, N_FOX, HEAD_DIM), 1.0),
        'cache_fox_v': nrm((DEPTH, n_pool, PAGE_SIZE, N_FOX, HEAD_DIM), 1.0),
        'cache_fox_logf': jax.nn.log_sigmoid(FORGET_BIAS_INIT + nrm((DEPTH, n_pool, PAGE_SIZE, N_FOX), 1.0)),
        'cache_dsa_k': nrm((DEPTH, n_pool, PAGE_SIZE, N_DSA, HEAD_DIM), 1.0),
        'cache_dsa_v': nrm((DEPTH, n_pool, PAGE_SIZE, N_DSA, HEAD_DIM), 1.0),
        'cache_idx_k': nrm((DEPTH, n_pool, PAGE_SIZE, IDX_DIM), 1.0),
        'page_table': page_table,
        'p_prompt': nrm((DEPTH, BATCH, SEQ, PLE_DIM), 1.0),
        'p_sample': nrm((DEPTH, DEC_BATCH, DEC_SEQ, PLE_DIM), 1.0),
        'rel_bias': nrm((NUM_BUCKETS, N_DSA), 0.5),
        'g_ffn1': gain((DEPTH, d)),
        'w1_pre': nrm((DEPTH, d, D_FF), d ** -0.5),
        'w3_pre': nrm((DEPTH, d, D_FF), d ** -0.5),
        'w2_pre': nrm((DEPTH, D_FF, d), D_FF ** -0.5),
        'g_mix': gain((DEPTH, d)),
        'w_in': nrm((DEPTH, d, N_IN), d ** -0.5),
        'b_forget': FORGET_BIAS_INIT + nrm((DEPTH, N_FOX), 0.1),
        'w_branch_fox': nrm((DEPTH, FOX_W, d), FOX_W ** -0.5),
        'w_branch_dsa': nrm((DEPTH, DSA_W, d), DSA_W ** -0.5),
        'w_out': nrm((DEPTH, d, d), d ** -0.5),
        'g_ffn2': gain((DEPTH, d)),
        'w1_post': nrm((DEPTH, d, D_FF), d ** -0.5),
        'w3_post': nrm((DEPTH, d, D_FF), d ** -0.5),
        'w2_post': nrm((DEPTH, D_FF, d), D_FF ** -0.5),
        'g_ple': gain((DEPTH, d)),
        'w_ple': nrm((DEPTH, PLE_DIM, d), PLE_DIM ** -0.5),
        'w_ple_gate': nrm((DEPTH, d, d), d ** -0.5),
        'g_final': gain((d,)),
    }


def reference(x_prompt, x_sample, cache_fox_k, cache_fox_v, cache_fox_logf, cache_dsa_k, cache_dsa_v,
              cache_idx_k, page_table, p_prompt, p_sample, rel_bias, g_ffn1, w1_pre, w3_pre, w2_pre,
              g_mix, w_in, b_forget, w_branch_fox, w_branch_dsa, w_out, g_ffn2, w1_post, w3_post,
              w2_post, g_ple, w_ple, w_ple_gate, g_final):
    xp, xs = x_prompt, x_sample
    st_p, st_s = [], []
    for l in range(DEPTH):
        lw = (g_ffn1[l], w1_pre[l], w3_pre[l], w2_pre[l], g_mix[l], w_in[l], b_forget[l],
              w_branch_fox[l], w_branch_dsa[l], w_out[l], g_ffn2[l], w1_post[l], w3_post[l],
              w2_post[l], g_ple[l], w_ple[l], w_ple_gate[l])
        mix_p = functools.partial(_mix_prompt, rel_bias=rel_bias)
        mix_s = functools.partial(_mix_sample, rel_bias=rel_bias, page_table=page_table,
                                  fox_k=cache_fox_k[l], fox_v=cache_fox_v[l], fox_lf=cache_fox_logf[l],
                                  dsa_k=cache_dsa_k[l], dsa_v=cache_dsa_v[l], idx_k=cache_idx_k[l])
        xp, sp = _block(xp, p_prompt[l], mix_p, *lw)
        xs, ss = _block(xs, p_sample[l], mix_s, *lw)
        st_p.append(sp)
        st_s.append(ss)
    pfk, pfv, pflf, pdk, pdv, pik = [jnp.stack(a) for a in zip(*st_p)]
    sfk, sfv, sflf, sdk, sdv, sik = [jnp.stack(a) for a in zip(*st_s)]
    y_prompt = _rms(xp, g_final)
    y_sample = _rms(xs, g_final)
    return (y_prompt, y_sample, pfk, pfv, pflf, pdk, pdv, pik, sfk, sfv, sflf, sdk, sdv, sik)
```

```python
import functools
import math

import numpy as np
import jax
import jax.numpy as jnp
from jax import lax
from jax.experimental import pallas as pl
from jax.experimental.pallas import tpu as pltpu

F32 = jnp.float32
BF16 = jnp.bfloat16
I32 = jnp.int32

TOPK_MAX = 256
MAX_DISTANCE = 128
RMS_EPS = 1e-6
IDX_HEADS = 16

V7X_VMEM_BYTES = 64 * 1024 * 1024
VMEM_LIMIT = V7X_VMEM_BYTES - 8 * 1024 * 1024
LANES = 128
SUBLANES = 8

NEG_INF = float("-inf")
M_INIT = -1e30
INT_MIN = -2 ** 31
NEG_INF_KEY = int(np.int32(np.uint32(0xFF800000) ^ np.uint32(0x7FFFFFFF)))


def _cparams(sem):
    return pltpu.CompilerParams(dimension_semantics=sem, vmem_limit_bytes=VMEM_LIMIT)


def _rms(x, g):
    ms = jnp.mean(x * x, axis=-1, keepdims=True)
    return x * lax.rsqrt(ms + RMS_EPS) * g


def _sigmoid(x):
    return 1.0 / (1.0 + jnp.exp(-x))


def _sort_key(x):
    bits = pltpu.bitcast(x, I32)
    return bits ^ ((bits >> 31) & jnp.int32(0x7FFFFFFF))


def _ffn_kernel(x_ref, g_ref, w1_ref, w3_ref, w2_ref, gn_ref, o_ref, hn_ref, h_sc, acc_sc):
    f = pl.program_id(1)

    @pl.when(f == 0)
    def _():
        h_sc[...] = _rms(x_ref[...], g_ref[...]).astype(BF16)
        acc_sc[...] = jnp.zeros_like(acc_sc)

    h = h_sc[...]
    a = jnp.dot(h, w1_ref[...], preferred_element_type=F32)
    b = jnp.dot(h, w3_ref[...], preferred_element_type=F32)
    act = (a * _sigmoid(a) * b).astype(BF16)
    acc_sc[...] += jnp.dot(act, w2_ref[...], preferred_element_type=F32)

    @pl.when(f == pl.num_programs(1) - 1)
    def _():
        x1 = x_ref[...] + 0.5 * acc_sc[...]
        o_ref[...] = x1
        hn_ref[...] = _rms(x1, gn_ref[...]).astype(BF16)


def _ffn(x, g, w1, w3, w2, g_next, *, tm=512, tf=512):
    m, d = x.shape
    dff = w1.shape[1]
    assert m % tm == 0 and dff % tf == 0
    return pl.pallas_call(
        _ffn_kernel,
        grid=(m // tm, dff // tf),
        in_specs=[
            pl.BlockSpec((tm, d), lambda i, f: (i, 0)),
            pl.BlockSpec((1, d), lambda i, f: (0, 0)),
            pl.BlockSpec((d, tf), lambda i, f: (0, f)),
            pl.BlockSpec((d, tf), lambda i, f: (0, f)),
            pl.BlockSpec((tf, d), lambda i, f: (f, 0)),
            pl.BlockSpec((1, d), lambda i, f: (0, 0)),
        ],
        out_specs=[
            pl.BlockSpec((tm, d), lambda i, f: (i, 0)),
            pl.BlockSpec((tm, d), lambda i, f: (i, 0)),
        ],
        out_shape=[jax.ShapeDtypeStruct((m, d), F32), jax.ShapeDtypeStruct((m, d), BF16)],
        scratch_shapes=[pltpu.VMEM((tm, d), BF16), pltpu.VMEM((tm, d), F32)],
        compiler_params=_cparams(("parallel", "arbitrary")),
        name="ffn",
    )(x, g.reshape(1, d), w1, w3, w2, g_next.reshape(1, d))


def _proj_qkv_kernel(h_ref, w_ref, o32_ref, o16_ref):
    z = jnp.dot(h_ref[...], w_ref[...], preferred_element_type=F32)
    o32_ref[...] = z
    o16_ref[...] = z.astype(BF16)


def _proj_gate_kernel(h_ref, w_ref, o_ref):
    z = jnp.dot(h_ref[...], w_ref[...], preferred_element_type=F32)
    o_ref[...] = _sigmoid(z)


def _proj_small_kernel(h_ref, w_ref, b_ref, o_ref, *, n_fox):
    z = jnp.dot(h_ref[...], w_ref[...], preferred_element_type=F32)
    lane = lax.broadcasted_iota(I32, z.shape, 1)
    zf = z + b_ref[...]
    logsig = jnp.minimum(zf, 0.0) - jnp.log1p(jnp.exp(-jnp.abs(zf)))
    o_ref[...] = jnp.where(lane < n_fox, logsig, z)


def _proj(kind, h, w, *, tm=512, tn=1024, bias=None, n_fox=0):
    m, k = h.shape
    n = w.shape[1]
    tn = min(tn, n)
    assert m % tm == 0 and n % tn == 0
    grid = (n // tn, m // tm)
    h_spec = pl.BlockSpec((tm, k), lambda j, i: (i, 0))
    w_spec = pl.BlockSpec((k, tn), lambda j, i: (0, j))
    o_spec = pl.BlockSpec((tm, tn), lambda j, i: (i, j))
    cp = _cparams(("parallel", "parallel"))
    if kind == "qkv":
        return pl.pallas_call(
            _proj_qkv_kernel, grid=grid, in_specs=[h_spec, w_spec], out_specs=[o_spec, o_spec],
            out_shape=[jax.ShapeDtypeStruct((m, n), F32), jax.ShapeDtypeStruct((m, n), BF16)],
            compiler_params=cp, name="proj_qkv")(h, w)
    if kind == "gate":
        return pl.pallas_call(
            _proj_gate_kernel, grid=grid, in_specs=[h_spec, w_spec], out_specs=o_spec,
            out_shape=jax.ShapeDtypeStruct((m, n), F32), compiler_params=cp, name="proj_gate")(h, w)
    assert kind == "small"
    b_spec = pl.BlockSpec((1, tn), lambda j, i: (0, j))
    return pl.pallas_call(
        functools.partial(_proj_small_kernel, n_fox=n_fox), grid=grid,
        in_specs=[h_spec, w_spec, b_spec], out_specs=o_spec,
        out_shape=jax.ShapeDtypeStruct((m, n), F32), compiler_params=cp, name="proj_small")(h, w, bias)


def _merge_kernel(oa_ref, ob_ref, g_ref, wa_ref, wb_ref, m_ref, *, d):
    ua = jnp.dot(oa_ref[...], wa_ref[...], preferred_element_type=F32)
    ub = jnp.dot(ob_ref[...], wb_ref[...], preferred_element_type=F32)
    m_ref[...] = (g_ref[:, :d] * ua + g_ref[:, d:] * ub).astype(BF16)


def _merge(oa, ob, gate, wa, wb, *, tm=256):
    m, wdt = oa.shape
    d = wa.shape[1]
    assert m % tm == 0
    return pl.pallas_call(
        functools.partial(_merge_kernel, d=d),
        grid=(m // tm,),
        in_specs=[
            pl.BlockSpec((tm, wdt), lambda i: (i, 0)),
            pl.BlockSpec((tm, wdt), lambda i: (i, 0)),
            pl.BlockSpec((tm, 2 * d), lambda i: (i, 0)),
            pl.BlockSpec((wdt, d), lambda i: (0, 0)),
            pl.BlockSpec((wdt, d), lambda i: (0, 0)),
        ],
        out_specs=pl.BlockSpec((tm, d), lambda i: (i, 0)),
        out_shape=jax.ShapeDtypeStruct((m, d), BF16),
        compiler_params=_cparams(("parallel",)),
        name="merge",
    )(oa, ob, gate, wa, wb)


def _resid_mm_kernel(x_ref, m_ref, w_ref, o_ref):
    o_ref[...] = x_ref[...] + jnp.dot(m_ref[...], w_ref[...], preferred_element_type=F32)


def _resid_mm(x, mm, w, *, tm=256):
    m, d = x.shape
    k = mm.shape[1]
    assert m % tm == 0
    return pl.pallas_call(
        _resid_mm_kernel,
        grid=(m // tm,),
        in_specs=[
            pl.BlockSpec((tm, d), lambda i: (i, 0)),
            pl.BlockSpec((tm, k), lambda i: (i, 0)),
            pl.BlockSpec((k, d), lambda i: (0, 0)),
        ],
        out_specs=pl.BlockSpec((tm, d), lambda i: (i, 0)),
        out_shape=jax.ShapeDtypeStruct((m, d), F32),
        compiler_params=_cparams(("parallel",)),
        name="resid_mm",
    )(x, mm, w)


def _ple_kernel(x_ref, p_ref, h_ref, wp_ref, wg_ref, gn_ref, o_ref, y_ref):
    e = jnp.dot(p_ref[...], wp_ref[...], preferred_element_type=F32)
    gt = jnp.dot(h_ref[...], wg_ref[...], preferred_element_type=F32)
    x4 = x_ref[...] + e * _sigmoid(gt)
    o_ref[...] = x4
    y_ref[...] = _rms(x4, gn_ref[...])


def _ple(x, p, h, wp, wg, g_next, *, tm=256):
    m, d = x.shape
    pd = p.shape[1]
    assert m % tm == 0
    return pl.pallas_call(
        _ple_kernel,
        grid=(m // tm,),
        in_specs=[
            pl.BlockSpec((tm, d), lambda i: (i, 0)),
            pl.BlockSpec((tm, pd), lambda i: (i, 0)),
            pl.BlockSpec((tm, d), lambda i: (i, 0)),
            pl.BlockSpec((pd, d), lambda i: (0, 0)),
            pl.BlockSpec((d, d), lambda i: (0, 0)),
            pl.BlockSpec((1, d), lambda i: (0, 0)),
        ],
        out_specs=[pl.BlockSpec((tm, d), lambda i: (i, 0)), pl.BlockSpec((tm, d), lambda i: (i, 0))],
        out_shape=[jax.ShapeDtypeStruct((m, d), F32), jax.ShapeDtypeStruct((m, d), F32)],
        compiler_params=_cparams(("parallel",)),
        name="ple",
    )(x, p, h, wp, wg, g_next.reshape(1, d))


def _cumsum_kernel(lf_ref, o_ref, carry_sc, *, n_heads, tb):
    @pl.when(pl.program_id(0) == 0)
    def _():
        carry_sc[...] = jnp.zeros_like(carry_sc)

    row = lax.broadcasted_iota(I32, (tb, tb), 0)
    col = lax.broadcasted_iota(I32, (tb, tb), 1)
    tri = jnp.where(col <= row, 1.0, 0.0).astype(F32)
    lf = lf_ref[...]
    for h in range(n_heads):
        lfb = jnp.broadcast_to(lf[:, h:h + 1], (tb, LANES))
        cum = jnp.dot(tri, lfb, preferred_element_type=F32, precision=lax.Precision.HIGHEST)
        cum = cum + carry_sc[h:h + 1, :]
        o_ref[h] = cum
        carry_sc[h:h + 1, :] = cum[tb - 1:tb, :]


def _cumsum_rep(lf, *, tb=256):
    s, n_heads = lf.shape
    assert s % tb == 0
    return pl.pallas_call(
        functools.partial(_cumsum_kernel, n_heads=n_heads, tb=tb),
        grid=(s // tb,),
        in_specs=[pl.BlockSpec((tb, n_heads), lambda i: (i, 0))],
        out_specs=pl.BlockSpec((n_heads, tb, LANES), lambda i: (0, i, 0)),
        out_shape=jax.ShapeDtypeStruct((n_heads, s, LANES), F32),
        scratch_shapes=[pltpu.VMEM((n_heads, LANES), F32)],
        compiler_params=_cparams(("arbitrary",)),
        name="cumsum_logf",
    )(lf)


def _idx_prompt_kernel(ki_ref, qt_ref, w_ref, mask_ref, ks_ref, *, n_sel, tq, n_heads, cb):
    i = pl.program_id(0)
    n_blocks = pl.num_programs(0)
    big = cb * tq

    @pl.when(i == 0)
    def _():
        ks_ref[...] = jnp.full(ks_ref.shape, NEG_INF_KEY, I32)

    qt = qt_ref[0]
    w = w_ref[0]
    row = lax.broadcasted_iota(I32, (tq, tq), 0)
    col = lax.broadcasted_iota(I32, (tq, tq), 1)

    def score_chunk(c, carry):
        r0 = pl.multiple_of(c * tq, tq)
        sc = jnp.dot(ki_ref[pl.ds(r0, tq), :], qt, preferred_element_type=F32)
        acc = jnp.zeros((tq, tq), F32)
        for h in range(n_heads):
            acc = acc + jnp.maximum(sc[:, h * tq:(h + 1) * tq], 0.0) * w[h:h + 1, :]
        acc = jnp.where(row + (c - i) * tq <= col, acc, NEG_INF)
        ks_ref[pl.ds(r0, tq), :] = _sort_key(acc)
        return carry

    lax.fori_loop(0, i + 1, score_chunk, 0)

    n_big = (i + cb) // cb

    def count_ge(cand):
        def body(c, acc):
            r0 = pl.multiple_of(c * big, big)
            hit = jnp.where(ks_ref[pl.ds(r0, big), :] >= cand, 1, 0).astype(I32)
            return acc + hit.reshape(big // SUBLANES, SUBLANES, tq).sum(axis=0)
        acc = lax.fori_loop(0, n_big, body, jnp.zeros((SUBLANES, tq), I32))
        return jnp.sum(acc, axis=0, keepdims=True)

    def bisect(b, ans_u):
        cand_u = ans_u | jnp.left_shift(jnp.int32(1), 31 - b)
        cnt = count_ge(cand_u ^ jnp.int32(INT_MIN))
        return jnp.where(cnt >= n_sel, cand_u, ans_u)

    ans_u = lax.fori_loop(0, 32, bisect, jnp.zeros((1, tq), I32))
    thr = ans_u ^ jnp.int32(INT_MIN)
    need = (n_sel - count_ge(thr + 1)).astype(F32)

    strict_lower = jnp.where(col < row, 1.0, 0.0).astype(BF16)

    def finish(c, seen):
        r0 = pl.multiple_of(c * tq, tq)
        key = ks_ref[pl.ds(r0, tq), :]
        eq = jnp.where(key == thr, 1.0, 0.0)
        before = jnp.dot(strict_lower, eq.astype(BF16), preferred_element_type=F32) + seen
        tie_ok = jnp.where(before < need, eq, 0.0)
        sel = jnp.where(key > thr, 1.0, tie_ok)
        mask_ref[pl.ds(r0, tq), :] = jnp.where(sel > 0.0, 0.0, NEG_INF).astype(BF16)
        return seen + jnp.sum(eq, axis=0, keepdims=True)

    lax.fori_loop(0, i + 1, finish, jnp.zeros((1, tq), F32))

    def fill(c, carry):
        r0 = pl.multiple_of(c * tq, tq)
        mask_ref[pl.ds(r0, tq), :] = jnp.full((tq, tq), NEG_INF, BF16)
        return carry

    lax.fori_loop(i + 1, n_blocks, fill, 0)


def _idx_prompt(ki, qt, w, *, n_sel, tq=128, cb=4):
    s, idim = ki.shape
    nb, n_heads, _ = w.shape
    assert nb * tq == s and (s // tq) % cb == 0 and cb * tq >= n_sel
    return pl.pallas_call(
        functools.partial(_idx_prompt_kernel, n_sel=n_sel, tq=tq, n_heads=n_heads, cb=cb),
        grid=(nb,),
        in_specs=[
            pl.BlockSpec((s, idim), lambda i: (0, 0)),
            pl.BlockSpec((1, idim, n_heads * tq), lambda i: (i, 0, 0)),
            pl.BlockSpec((1, n_heads, tq), lambda i: (i, 0, 0)),
        ],
        out_specs=pl.BlockSpec((s, tq), lambda i: (0, i)),
        out_shape=jax.ShapeDtypeStruct((s, s), BF16),
        scratch_shapes=[pltpu.VMEM((s, tq), I32)],
        compiler_params=_cparams(("arbitrary",)),
        name="idx_prompt",
    )(ki, qt, w)


def _attn_prompt_kernel(it_ref, jt_ref, qt_ref, k_ref, vt_ref, b1_ref, b2_ref, o_ref,
                        m_sc, l_sc, acc_sc, *, mode, t, n_heads, hd, scale):
    s = pl.program_id(0)
    i = it_ref[s]
    j = jt_ref[s]

    @pl.when(j == 0)
    def _():
        m_sc[...] = jnp.full(m_sc.shape, M_INIT, F32)
        l_sc[...] = jnp.zeros_like(l_sc)
        acc_sc[...] = jnp.zeros_like(acc_sc)

    if mode == "fox":
        tile_bias = b2_ref[jnp.minimum(i - j, 1)]
    else:
        sel_bias = b1_ref[...].astype(F32)
        near = jnp.minimum(i - j, 2)

    for h in range(n_heads):
        hs = slice(h * hd, (h + 1) * hd)
        st = jnp.dot(k_ref[:, hs], qt_ref[hs, :], preferred_element_type=F32) * scale
        if mode == "fox":
            fk = b1_ref[h]
            st = st - jnp.concatenate([fk] * (t // LANES), axis=1) + tile_bias
        else:
            st = st + b2_ref[h, near] + sel_bias
        m_old = m_sc[h:h + 1, :]
        m_new = jnp.maximum(m_old, jnp.max(st, axis=0, keepdims=True))
        alpha = jnp.exp(m_old - m_new)
        p = jnp.exp(st - m_new)
        l_sc[h:h + 1, :] = alpha * l_sc[h:h + 1, :] + jnp.sum(p, axis=0, keepdims=True)
        pv = jnp.dot(vt_ref[hs, :], p.astype(BF16), preferred_element_type=F32)
        acc_sc[hs, :] = alpha * acc_sc[hs, :] + pv
        m_sc[h:h + 1, :] = m_new

    @pl.when(j == i)
    def _():
        for h in range(n_heads):
            hs = slice(h * hd, (h + 1) * hd)
            o_ref[hs, :] = (acc_sc[hs, :] / l_sc[h:h + 1, :]).astype(BF16)


def _attn_prompt(mode, qt, k, vt, b1, b2, *, t, n_heads, hd):
    width, s = qt.shape
    nb = s // t
    assert nb * t == s and t % LANES == 0
    pairs = [(i, j) for i in range(nb) for j in range(i + 1)]
    it = jnp.asarray([p[0] for p in pairs], I32)
    jt = jnp.asarray([p[1] for p in pairs], I32)
    if mode == "fox":
        b1_spec = pl.BlockSpec((n_heads, t, LANES), lambda g, it, jt: (0, jt[g], 0))
    else:
        b1_spec = pl.BlockSpec((t, t), lambda g, it, jt: (jt[g], it[g]))
    grid_spec = pltpu.PrefetchScalarGridSpec(
        num_scalar_prefetch=2,
        grid=(len(pairs),),
        in_specs=[
            pl.BlockSpec((width, t), lambda g, it, jt: (0, it[g])),
            pl.BlockSpec((t, width), lambda g, it, jt: (jt[g], 0)),
            pl.BlockSpec((width, t), lambda g, it, jt: (0, jt[g])),
            b1_spec,
            pl.BlockSpec(memory_space=pltpu.VMEM),
        ],
        out_specs=pl.BlockSpec((width, t), lambda g, it, jt: (0, it[g])),
        scratch_shapes=[
            pltpu.VMEM((n_heads, t), F32),
            pltpu.VMEM((n_heads, t), F32),
            pltpu.VMEM((width, t), F32),
        ],
    )
    return pl.pallas_call(
        functools.partial(_attn_prompt_kernel, mode=mode, t=t, n_heads=n_heads, hd=hd,
                          scale=hd ** -0.5),
        grid_spec=grid_spec,
        out_shape=jax.ShapeDtypeStruct((width, s), BF16),
        compiler_params=_cparams(("arbitrary",)),
        name="attn_prompt_" + mode,
    )(it, jt, qt, k, vt, b1, b2)


def _idx_sample_kernel(pt_ref, *refs, n_sel, pps, page, n_steps, t_new, n_heads):
    page_refs = refs[:pps]
    qi_ref, w_ref, kn_ref, mask_ref, sc_sc = refs[pps:]
    st = pl.program_id(1)
    span = pps * page
    width = sc_sc.shape[1]
    n_past = n_steps * span
    qi = qi_ref[0]
    w = w_ref[0]

    def scores(kb):
        sc = lax.dot_general(qi, kb, (((1,), (1,)), ((), ())), preferred_element_type=F32)
        sc = jnp.maximum(sc, 0.0) * jnp.concatenate([w] * (kb.shape[0] // LANES), axis=1)
        acc = sc[0:SUBLANES, :]
        for h in range(1, n_heads):
            acc = acc + sc[h * SUBLANES:(h + 1) * SUBLANES, :]
        return acc

    kb = jnp.concatenate([r[0].astype(BF16) for r in page_refs], axis=0)
    c0 = pl.multiple_of(st * span, span)
    sc_sc[:, pl.ds(c0, span)] = _sort_key(scores(kb))

    @pl.when(st == n_steps - 1)
    def _():
        qrow = lax.broadcasted_iota(I32, (SUBLANES, page), 0)
        jcol = lax.broadcasted_iota(I32, (SUBLANES, page), 1)
        new = jnp.where((jcol <= qrow) & (jcol < t_new), scores(kn_ref[0]), NEG_INF)
        sc_sc[:, n_past:n_past + page] = _sort_key(new)
        keys = sc_sc[...]

        def bisect(b, ans_u):
            cand_u = ans_u | jnp.left_shift(jnp.int32(1), 31 - b)
            cand = cand_u ^ jnp.int32(INT_MIN)
            cnt = jnp.sum(jnp.where(keys >= cand, 1, 0).astype(I32), axis=1, keepdims=True)
            return jnp.where(cnt >= n_sel, cand_u, ans_u)

        ans_u = lax.fori_loop(0, 32, bisect, jnp.zeros((SUBLANES, 1), I32))
        thr = ans_u ^ jnp.int32(INT_MIN)
        n_gt = jnp.sum(jnp.where(keys > thr, 1, 0).astype(I32), axis=1, keepdims=True)
        need = (n_sel - n_gt).astype(F32)

        r = lax.broadcasted_iota(I32, (LANES, LANES), 0)
        c = lax.broadcasted_iota(I32, (LANES, LANES), 1)
        strict_upper = jnp.where(r < c, 1.0, 0.0).astype(BF16)
        seen = jnp.zeros((SUBLANES, 1), F32)
        for blk in range(width // LANES):
            kblk = keys[:, blk * LANES:(blk + 1) * LANES]
            eq = jnp.where(kblk == thr, 1.0, 0.0)
            before = jnp.dot(eq.astype(BF16), strict_upper, preferred_element_type=F32) + seen
            tie_ok = jnp.where(before < need, eq, 0.0)
            sel = jnp.where(kblk > thr, 1.0, tie_ok)
            mask_ref[0, :, blk * LANES:(blk + 1) * LANES] = jnp.where(sel > 0.0, 0.0, NEG_INF)
            seen = seen + jnp.sum(eq, axis=1, keepdims=True)


def _idx_sample(pt_flat, pool_ik, qi, w, kn, *, n_sel, n_pages, t_new, n_heads, pps=4):
    b = qi.shape[0]
    page, idim = pool_ik.shape[1:]
    assert n_pages % pps == 0 and page == LANES
    n_steps = n_pages // pps
    width = n_pages * page + page

    def page_spec(r):
        return pl.BlockSpec((1, page, idim),
                            lambda bi, st, pt: (pt[bi * n_pages + st * pps + r], 0, 0))

    grid_spec = pltpu.PrefetchScalarGridSpec(
        num_scalar_prefetch=1,
        grid=(b, n_steps),
        in_specs=[page_spec(r) for r in range(pps)] + [
            pl.BlockSpec((1, n_heads * SUBLANES, idim), lambda bi, st, pt: (bi, 0, 0)),
            pl.BlockSpec((1, n_heads * SUBLANES, LANES), lambda bi, st, pt: (bi, 0, 0)),
            pl.BlockSpec((1, page, idim), lambda bi, st, pt: (bi, 0, 0)),
        ],
        out_specs=pl.BlockSpec((1, SUBLANES, width), lambda bi, st, pt: (bi, 0, 0)),
        scratch_shapes=[pltpu.VMEM((SUBLANES, width), I32)],
    )
    return pl.pallas_call(
        functools.partial(_idx_sample_kernel, n_sel=n_sel, pps=pps, page=page, n_steps=n_steps,
                          t_new=t_new, n_heads=n_heads),
        grid_spec=grid_spec,
        out_shape=jax.ShapeDtypeStruct((b, SUBLANES, width), F32),
        compiler_params=_cparams(("arbitrary", "arbitrary")),
        name="idx_sample",
    )(pt_flat, *([pool_ik] * pps), qi, w, kn)


def _attn_sample_kernel(pt_ref, *refs, mode, pps, page, n_steps, t_new, n_heads, hd, scale):
    k_refs = refs[:pps]
    v_refs = refs[pps:2 * pps]
    rest = refs[2 * pps:]
    if mode == "fox":
        lf_refs = rest[:pps]
        q_ref, kn_ref, vn_ref, lfn_ref, o_ref, m_sc, l_sc, acc_sc, carry_sc = rest[pps:]
    else:
        q_ref, kn_ref, vn_ref, mask_ref, t5_ref, o_ref, m_sc, l_sc, acc_sc = rest
    st = pl.program_id(1)
    rows = t_new * n_heads
    n_past = n_steps * pps * page

    @pl.when(st == 0)
    def _():
        m_sc[...] = jnp.full(m_sc.shape, M_INIT, F32)
        l_sc[...] = jnp.zeros_like(l_sc)
        acc_sc[...] = jnp.zeros_like(acc_sc)
        if mode == "fox":
            carry_sc[...] = jnp.zeros_like(carry_sc)

    q = q_ref[0]
    if mode == "fox":
        r8 = lax.broadcasted_iota(I32, (rows, n_heads), 0)
        c8 = lax.broadcasted_iota(I32, (rows, n_heads), 1)
        head_of_row = jnp.where(r8 % n_heads == c8, 1.0, 0.0).astype(F32)
        ru = lax.broadcasted_iota(I32, (page, page), 0)
        cu = lax.broadcasted_iota(I32, (page, page), 1)
        upper_incl = jnp.where(ru <= cu, 1.0, 0.0).astype(F32)
        ones = jnp.ones((page, page), F32)

    def block(kb, vb, bias):
        s = lax.dot_general(q, kb, (((1,), (1,)), ((), ())), preferred_element_type=F32)
        s = s * scale + bias
        m_old = m_sc[...]
        m_new = jnp.maximum(m_old, jnp.max(s, axis=1, keepdims=True))
        alpha = jnp.exp(m_old - m_new)
        p = jnp.exp(s - m_new)
        l_sc[...] = alpha * l_sc[...] + jnp.sum(p, axis=1, keepdims=True)
        acc_sc[...] = alpha * acc_sc[...] + jnp.dot(p.astype(BF16), vb, preferred_element_type=F32)
        m_sc[...] = m_new

    def fox_bias(lf):
        lft = lax.dot_general(head_of_row, lf, (((1,), (1,)), ((), ())),
                              preferred_element_type=F32, precision=lax.Precision.HIGHEST)
        cum = jnp.dot(lft, upper_incl, preferred_element_type=F32,
                      precision=lax.Precision.HIGHEST) + carry_sc[...]
        carry_sc[...] += jnp.dot(lft, ones, preferred_element_type=F32,
                                 precision=lax.Precision.HIGHEST)
        return -cum

    def mask_rows(mk):
        return jnp.concatenate(
            [jnp.broadcast_to(mk[qq:qq + 1, :], (n_heads, mk.shape[1])) for qq in range(t_new)], axis=0)

    for r in range(pps):
        kb = k_refs[r][0].astype(BF16)
        vb = v_refs[r][0].astype(BF16)
        if mode == "fox":
            bias = fox_bias(lf_refs[r][0])
        else:
            c0 = pl.multiple_of((st * pps + r) * page, page)
            bias = t5_ref[:, pl.ds(c0, page)] + mask_rows(mask_ref[0, :, pl.ds(c0, page)])
        block(kb, vb, bias)

    @pl.when(st == n_steps - 1)
    def _():
        if mode == "fox":
            qrow = lax.broadcasted_iota(I32, (rows, page), 0) // n_heads
            jcol = lax.broadcasted_iota(I32, (rows, page), 1)
            bias = jnp.where((jcol <= qrow) & (jcol < t_new), fox_bias(lfn_ref[0]), NEG_INF)
        else:
            bias = t5_ref[:, n_past:n_past + page] + mask_rows(mask_ref[0, :, n_past:n_past + page])
        block(kn_ref[0], vn_ref[0], bias)
        acc = acc_sc[...]
        hrow = lax.broadcasted_iota(I32, (rows, hd), 0) % n_heads
        out = jnp.zeros((rows, hd), F32)
        for h in range(n_heads):
            out = out + jnp.where(hrow == h, acc[:, h * hd:(h + 1) * hd], 0.0)
        o_ref[0] = (out / l_sc[...]).astype(BF16)


def _attn_sample(mode, pt_flat, pool_k, pool_v, qbd, kn, vn, extra, *, n_pages, t_new, n_heads, hd, pps=4):
    b, rows, width = qbd.shape
    page = pool_k.shape[1]
    assert n_pages % pps == 0 and page == LANES and rows == t_new * n_heads
    n_steps = n_pages // pps

    def page_spec(shape_tail, r):
        zeros = (0,) * len(shape_tail)
        return pl.BlockSpec((1,) + shape_tail,
                            lambda bi, st, pt: (pt[bi * n_pages + st * pps + r],) + zeros)

    def per_req(shape_tail):
        zeros = (0,) * len(shape_tail)
        return pl.BlockSpec((1,) + shape_tail, lambda bi, st, pt: (bi,) + zeros)

    in_specs = [page_spec((page, width), r) for r in range(pps)] * 2
    args = [pool_k] * pps + [pool_v] * pps
    scratch = [pltpu.VMEM((rows, 1), F32), pltpu.VMEM((rows, 1), F32), pltpu.VMEM((rows, width), F32)]
    if mode == "fox":
        pool_lf, lfn = extra
        in_specs += [page_spec((page, n_heads), r) for r in range(pps)]
        args += [pool_lf] * pps
        in_specs += [per_req((rows, width)), per_req((page, width)), per_req((page, width)),
                     per_req((page, n_heads))]
        args += [qbd, kn, vn, lfn]
        scratch.append(pltpu.VMEM((rows, page), F32))
    else:
        mask, t5 = extra
        in_specs += [per_req((rows, width)), per_req((page, width)), per_req((page, width)),
                     per_req(mask.shape[1:]), pl.BlockSpec(memory_space=pltpu.VMEM)]
        args += [qbd, kn, vn, mask, t5]
    grid_spec = pltpu.PrefetchScalarGridSpec(
        num_scalar_prefetch=1,
        grid=(b, n_steps),
        in_specs=in_specs,
        out_specs=pl.BlockSpec((1, rows, hd), lambda bi, st, pt: (bi, 0, 0)),
        scratch_shapes=scratch,
    )
    return pl.pallas_call(
        functools.partial(_attn_sample_kernel, mode=mode, pps=pps, page=page, n_steps=n_steps,
                          t_new=t_new, n_heads=n_heads, hd=hd, scale=hd ** -0.5),
        grid_spec=grid_spec,
        out_shape=jax.ShapeDtypeStruct((b, rows, hd), BF16),
        compiler_params=_cparams(("arbitrary", "arbitrary")),
        name="attn_sample_" + mode,
    )(pt_flat, *args)


def _t5_bucket(dist, num_buckets):
    n = jnp.maximum(dist, 0)
    exact = num_buckets // 2
    nf = jnp.maximum(n, 1).astype(F32)
    large = exact + (jnp.log(nf / exact) / math.log(MAX_DISTANCE / exact)
                     * (num_buckets - exact)).astype(I32)
    return jnp.where(n < exact, n, jnp.minimum(large, num_buckets - 1))


def _t5_bias(rel_bias, dist):
    b = jnp.moveaxis(rel_bias[_t5_bucket(dist, rel_bias.shape[0])].astype(F32), -1, 0)
    return jnp.where(dist[None] >= 0, b, NEG_INF)


def _is_pow2(x):
    return math.frexp(x)[0] == 0.5


def _mix_prompt(q_a, k_a, v_a, lf_a, q_b, k_b, v_b, q_i, k_i, w_i, rel_bias, *, n_heads, hd, t=256):
    s = q_a.shape[0]
    idim = k_i.shape[1]
    ih = w_i.shape[1]
    n_sel = min(TOPK_MAX, s // 4)
    assert MAX_DISTANCE <= t + 1
    idx_scale, idx_w_scale = idim ** -0.5, ih ** -0.5
    assert _is_pow2(idx_scale) and _is_pow2(idx_w_scale)

    fk = _cumsum_rep(lf_a)
    r = jnp.arange(t)
    dist0 = r[None, :] - r[:, None]
    causal = jnp.where(dist0 >= 0, 0.0, NEG_INF).astype(F32)
    ctiles = jnp.stack([causal, jnp.zeros_like(causal)])
    o_a = _attn_prompt("fox", q_a.T, k_a, v_a.T, fk, ctiles, t=t, n_heads=n_heads, hd=hd).T

    tq = LANES
    nb = s // tq
    qt = q_i.reshape(nb, tq, ih, idim).transpose(0, 3, 2, 1).reshape(nb, idim, ih * tq)
    w = (w_i * (idx_scale * idx_w_scale)).reshape(nb, tq, ih).transpose(0, 2, 1)
    mask = _idx_prompt(k_i, qt, w, n_sel=n_sel, tq=tq)
    btiles = jnp.stack([_t5_bias(rel_bias, dist0 + nd * t) for nd in range(3)], axis=1)
    o_b = _attn_prompt("dsa", q_b.T, k_b, v_b.T, mask, btiles, t=t, n_heads=n_heads, hd=hd).T
    return o_a, o_b


def _block_diag_q(q, n_heads, hd):
    b, t_new, _ = q.shape
    qh = q.reshape(b, t_new, n_heads, hd)
    eye = jnp.eye(n_heads, dtype=q.dtype)
    return jnp.einsum("bthd,hg->bthgd", qh, eye).reshape(b, t_new * n_heads, n_heads * hd)


def _pad_rows(x, rows):
    return jnp.pad(x, ((0, 0), (0, rows - x.shape[1]), (0, 0)))


def _mix_sample(q_a, k_a, v_a, lf_a, q_b, k_b, v_b, q_i, k_i, w_i, rel_bias, page_table,
                fox_k, fox_v, fox_lf, dsa_k, dsa_v, idx_k, *, n_heads, hd):
    b, t_new, width = q_a.shape
    n_pages = page_table.shape[1]
    n_pool, page = fox_k.shape[:2]
    idim = k_i.shape[2]
    ih = w_i.shape[2]
    past = n_pages * page
    n_sel = min(TOPK_MAX, (past + t_new) // 4)
    idx_scale, idx_w_scale = idim ** -0.5, ih ** -0.5
    assert _is_pow2(idx_scale) and _is_pow2(idx_w_scale)
    assert t_new <= SUBLANES
    pt_flat = page_table.reshape(-1).astype(I32)

    kn_a, vn_a = _pad_rows(k_a, page), _pad_rows(v_a, page)
    lfn = _pad_rows(lf_a, page)
    o_a = _attn_sample("fox", pt_flat, fox_k.reshape(n_pool, page, width), fox_v.reshape(n_pool, page, width),
                       _block_diag_q(q_a, n_heads, hd), kn_a, vn_a, (fox_lf, lfn),
                       n_pages=n_pages, t_new=t_new, n_heads=n_heads, hd=hd)

    qi = jnp.pad(q_i.reshape(b, t_new, ih, idim), ((0, 0), (0, SUBLANES - t_new), (0, 0), (0, 0)))
    qi = qi.transpose(0, 2, 1, 3).reshape(b, ih * SUBLANES, idim)
    wr = jnp.pad(w_i * (idx_scale * idx_w_scale), ((0, 0), (0, SUBLANES - t_new), (0, 0)))
    wr = jnp.broadcast_to(wr.transpose(0, 2, 1).reshape(b, ih * SUBLANES, 1), (b, ih * SUBLANES, LANES))
    mask = _idx_sample(pt_flat, idx_k, qi, wr, _pad_rows(k_i, page), n_sel=n_sel, n_pages=n_pages,
                       t_new=t_new, n_heads=ih)
    kpos = jnp.arange(past + page)
    qpos = past + jnp.arange(t_new)
    dist = jnp.where(kpos[None, :] < past + t_new, qpos[:, None] - kpos[None, :], -1)
    t5 = _t5_bias(rel_bias, dist)
    t5 = t5.transpose(1, 0, 2).reshape(t_new * n_heads, past + page)
    o_b = _attn_sample("dsa", pt_flat, dsa_k.reshape(n_pool, page, width), dsa_v.reshape(n_pool, page, width),
                       _block_diag_q(q_b, n_heads, hd), _pad_rows(k_b, page), _pad_rows(v_b, page),
                       (mask, t5), n_pages=n_pages, t_new=t_new, n_heads=n_heads, hd=hd)
    return o_a.reshape(b * t_new, width), o_b.reshape(b * t_new, width)


def kernel(x_prompt, x_sample, cache_fox_k, cache_fox_v, cache_fox_logf, cache_dsa_k, cache_dsa_v, cache_idx_k, page_table, p_prompt, p_sample, rel_bias, g_ffn1, w1_pre, w3_pre, w2_pre, g_mix, w_in, b_forget, w_branch_fox, w_branch_dsa, w_out, g_ffn2, w1_post, w3_post, w2_post, g_ple, w_ple, w_ple_gate, g_final):
    depth = w_in.shape[0]
    bp, sp, d = x_prompt.shape
    bs, ts, _ = x_sample.shape
    assert bp == 1
    n_fox, hd = cache_fox_k.shape[3:]
    n_dsa = cache_dsa_k.shape[3]
    assert n_fox == n_dsa
    idim = cache_idx_k.shape[-1]
    ih = IDX_HEADS
    fw, dw = n_fox * hd, n_dsa * hd
    n_in = w_in.shape[2]
    assert n_in == 3 * fw + n_fox + 3 * dw + ih * idim + idim + ih + 2 * d
    mp, ms = bp * sp, bs * ts
    small_w = n_fox + idim + ih
    assert small_w <= LANES

    x = jnp.concatenate([x_prompt.reshape(mp, d), x_sample.reshape(ms, d)], axis=0)
    st_p, st_s = [], []
    for l in range(depth):
        offs = np.cumsum([0, fw, fw, fw, n_fox, dw, dw, dw, ih * idim, idim, ih, 2 * d])
        col = lambda a, c: w_in[l][:, offs[a]:offs[a] + c]
        w_qkv = jnp.concatenate([col(0, fw), col(4, dw), col(7, ih * idim),
                                 col(1, fw), col(2, fw), col(5, dw), col(6, dw)], axis=1).astype(BF16)
        w_gate = col(10, 2 * d).astype(BF16)
        w_small = jnp.concatenate([col(3, n_fox), col(8, idim), col(9, ih)], axis=1)
        w_small = jnp.pad(w_small, ((0, 0), (0, LANES - small_w))).astype(BF16)
        b_small = jnp.pad(b_forget[l], (0, LANES - n_fox)).reshape(1, LANES)

        x1, h_mix = _ffn(x, g_ffn1[l], w1_pre[l].astype(BF16), w3_pre[l].astype(BF16),
                         w2_pre[l].astype(BF16), g_mix[l])
        z32, z16 = _proj("qkv", h_mix, w_qkv)
        gate = _proj("gate", h_mix, w_gate)
        small = _proj("small", h_mix, w_small, bias=b_small, n_fox=n_fox)

        q_a, q_b, q_i = z16[:, :fw], z16[:, fw:fw + dw], z16[:, fw + dw:fw + dw + ih * idim]
        kv0 = fw + dw + ih * idim
        kv16 = [z16[:, kv0 + n * fw:kv0 + (n + 1) * fw] for n in range(4)]
        kv32 = [z32[:, kv0 + n * fw:kv0 + (n + 1) * fw] for n in range(4)]
        lf = small[:, :n_fox]
        k_i32 = small[:, n_fox:n_fox + idim]
        w_i = small[:, n_fox + idim:small_w]
        k_i16 = k_i32.astype(BF16)

        o_ap, o_bp = _mix_prompt(q_a[:mp], kv16[0][:mp], kv16[1][:mp], lf[:mp], q_b[:mp], kv16[2][:mp],
                                 kv16[3][:mp], q_i[:mp], k_i16[:mp], w_i[:mp], rel_bias,
                                 n_heads=n_fox, hd=hd)
        rs = lambda a: a[mp:].reshape(bs, ts, -1)
        o_as, o_bs = _mix_sample(rs(q_a), rs(kv16[0]), rs(kv16[1]), rs(lf), rs(q_b), rs(kv16[2]),
                                 rs(kv16[3]), rs(q_i), rs(k_i16), rs(w_i), rel_bias, page_table,
                                 cache_fox_k[l], cache_fox_v[l], cache_fox_logf[l], cache_dsa_k[l],
                                 cache_dsa_v[l], cache_idx_k[l], n_heads=n_fox, hd=hd)
        o_a = jnp.concatenate([o_ap, o_as], axis=0)
        o_b = jnp.concatenate([o_bp, o_bs], axis=0)

        merged = _merge(o_a, o_b, gate, w_branch_fox[l].astype(BF16), w_branch_dsa[l].astype(BF16))
        x2 = _resid_mm(x1, merged, w_out[l].astype(BF16))
        x3, h_ple = _ffn(x2, g_ffn2[l], w1_post[l].astype(BF16), w3_post[l].astype(BF16),
                         w2_post[l].astype(BF16), g_ple[l])
        p = jnp.concatenate([p_prompt[l].reshape(mp, -1), p_sample[l].reshape(ms, -1)], axis=0).astype(BF16)
        x, y = _ple(x3, p, h_ple, w_ple[l].astype(BF16), w_ple_gate[l].astype(BF16), g_final)

        heads = lambda a, bb, tt: a.reshape(bb, tt, n_fox, hd)
        st_p.append((heads(kv32[0][:mp], bp, sp), heads(kv32[1][:mp], bp, sp), lf[:mp].reshape(bp, sp, n_fox),
                     heads(kv32[2][:mp], bp, sp), heads(kv32[3][:mp], bp, sp), k_i32[:mp].reshape(bp, sp, idim)))
        st_s.append((heads(kv32[0][mp:], bs, ts), heads(kv32[1][mp:], bs, ts), lf[mp:].reshape(bs, ts, n_fox),
                     heads(kv32[2][mp:], bs, ts), heads(kv32[3][mp:], bs, ts), k_i32[mp:].reshape(bs, ts, idim)))

    pfk, pfv, pflf, pdk, pdv, pik = [jnp.stack(a) for a in zip(*st_p)]
    sfk, sfv, sflf, sdk, sdv, sik = [jnp.stack(a) for a in zip(*st_s)]
    y_prompt = y[:mp].reshape(bp, sp, d)
    y_sample = y[mp:].reshape(bs, ts, d)
    return (y_prompt, y_sample, pfk, pfv, pflf, pdk, pdv, pik, sfk, sfv, sflf, sdk, sdv, sik)
```

```python
import functools
import math

import numpy as np
import jax
import jax.numpy as jnp
from jax import lax
from jax.experimental import pallas as pl
from jax.experimental.pallas import tpu as pltpu

F32 = jnp.float32
BF16 = jnp.bfloat16
I32 = jnp.int32

TOPK_MAX = 256
MAX_DISTANCE = 128
RMS_EPS = 1e-6
IDX_HEADS = 16

V7X_VMEM_BYTES = 64 * 1024 * 1024
VMEM_LIMIT = V7X_VMEM_BYTES - 8 * 1024 * 1024
LANES = 128
SUBLANES = 8

NEG_INF = float("-inf")
M_INIT = -1e30
INT_MIN = -2 ** 31
NEG_INF_KEY = int(np.int32(np.uint32(0xFF800000) ^ np.uint32(0x7FFFFFFF)))


def _cparams(sem):
    return pltpu.CompilerParams(dimension_semantics=sem, vmem_limit_bytes=VMEM_LIMIT)


def _rms(x, g):
    ms = jnp.mean(x * x, axis=-1, keepdims=True)
    return x * lax.rsqrt(ms + RMS_EPS) * g


def _sigmoid(x):
    return 1.0 / (1.0 + jnp.exp(-x))


def _split3(x):
    hi = x.astype(BF16)
    r1 = x - hi.astype(F32)
    mid = r1.astype(BF16)
    lo = (r1 - mid.astype(F32)).astype(BF16)
    return hi, mid, lo


def _sort_key(x):
    bits = pltpu.bitcast(x, I32)
    return bits ^ ((bits >> 31) & jnp.int32(0x7FFFFFFF))


def _ffn_kernel(x_ref, g_ref, w1_ref, w3_ref, w2_ref, gn_ref, o_ref, hn_ref, h_sc, acc_sc):
    f = pl.program_id(1)

    @pl.when(f == 0)
    def _():
        h_sc[...] = _rms(x_ref[...], g_ref[...]).astype(BF16)
        acc_sc[...] = jnp.zeros_like(acc_sc)

    h = h_sc[...]
    a = jnp.dot(h, w1_ref[...], preferred_element_type=F32)
    b = jnp.dot(h, w3_ref[...], preferred_element_type=F32)
    act = (a * _sigmoid(a) * b).astype(BF16)
    acc_sc[...] += jnp.dot(act, w2_ref[...], preferred_element_type=F32)

    @pl.when(f == pl.num_programs(1) - 1)
    def _():
        x1 = x_ref[...] + 0.5 * acc_sc[...]
        o_ref[...] = x1
        hn_ref[...] = _rms(x1, gn_ref[...]).astype(BF16)


def _ffn(x, g, w1, w3, w2, g_next, *, tm=512, tf=512):
    m, d = x.shape
    dff = w1.shape[1]
    assert m % tm == 0 and dff % tf == 0
    return pl.pallas_call(
        _ffn_kernel,
        grid=(m // tm, dff // tf),
        in_specs=[
            pl.BlockSpec((tm, d), lambda i, f: (i, 0)),
            pl.BlockSpec((1, d), lambda i, f: (0, 0)),
            pl.BlockSpec((d, tf), lambda i, f: (0, f)),
            pl.BlockSpec((d, tf), lambda i, f: (0, f)),
            pl.BlockSpec((tf, d), lambda i, f: (f, 0)),
            pl.BlockSpec((1, d), lambda i, f: (0, 0)),
        ],
        out_specs=[
            pl.BlockSpec((tm, d), lambda i, f: (i, 0)),
            pl.BlockSpec((tm, d), lambda i, f: (i, 0)),
        ],
        out_shape=[jax.ShapeDtypeStruct((m, d), F32), jax.ShapeDtypeStruct((m, d), BF16)],
        scratch_shapes=[pltpu.VMEM((tm, d), BF16), pltpu.VMEM((tm, d), F32)],
        compiler_params=_cparams(("parallel", "arbitrary")),
        name="ffn",
    )(x, g.reshape(1, d), w1, w3, w2, g_next.reshape(1, d))


def _proj_qkv_kernel(h_ref, w_ref, o32_ref, o16_ref):
    z = jnp.dot(h_ref[...], w_ref[...], preferred_element_type=F32)
    o32_ref[...] = z
    o16_ref[...] = z.astype(BF16)


def _proj_gate_kernel(h_ref, w_ref, o_ref):
    z = jnp.dot(h_ref[...], w_ref[...], preferred_element_type=F32)
    o_ref[...] = _sigmoid(z)


def _proj_small_kernel(h_ref, w_ref, b_ref, o_ref, *, n_fox):
    z = jnp.dot(h_ref[...], w_ref[...], preferred_element_type=F32)
    lane = lax.broadcasted_iota(I32, z.shape, 1)
    zf = z + b_ref[...]
    logsig = jnp.minimum(zf, 0.0) - jnp.log1p(jnp.exp(-jnp.abs(zf)))
    o_ref[...] = jnp.where(lane < n_fox, logsig, z)


def _proj(kind, h, w, *, tm=512, tn=1024, bias=None, n_fox=0):
    m, k = h.shape
    n = w.shape[1]
    tn = min(tn, n)
    assert m % tm == 0 and n % tn == 0
    grid = (n // tn, m // tm)
    h_spec = pl.BlockSpec((tm, k), lambda j, i: (i, 0))
    w_spec = pl.BlockSpec((k, tn), lambda j, i: (0, j))
    o_spec = pl.BlockSpec((tm, tn), lambda j, i: (i, j))
    cp = _cparams(("parallel", "parallel"))
    if kind == "qkv":
        return pl.pallas_call(
            _proj_qkv_kernel, grid=grid, in_specs=[h_spec, w_spec], out_specs=[o_spec, o_spec],
            out_shape=[jax.ShapeDtypeStruct((m, n), F32), jax.ShapeDtypeStruct((m, n), BF16)],
            compiler_params=cp, name="proj_qkv")(h, w)
    if kind == "gate":
        return pl.pallas_call(
            _proj_gate_kernel, grid=grid, in_specs=[h_spec, w_spec], out_specs=o_spec,
            out_shape=jax.ShapeDtypeStruct((m, n), F32), compiler_params=cp, name="proj_gate")(h, w)
    assert kind == "small"
    b_spec = pl.BlockSpec((1, tn), lambda j, i: (0, j))
    return pl.pallas_call(
        functools.partial(_proj_small_kernel, n_fox=n_fox), grid=grid,
        in_specs=[h_spec, w_spec, b_spec], out_specs=o_spec,
        out_shape=jax.ShapeDtypeStruct((m, n), F32), compiler_params=cp, name="proj_small")(h, w, bias)


def _merge_kernel(oa_ref, ob_ref, g_ref, wa_ref, wb_ref, m_ref, *, d):
    ua = jnp.dot(oa_ref[...], wa_ref[...], preferred_element_type=F32)
    ub = jnp.dot(ob_ref[...], wb_ref[...], preferred_element_type=F32)
    m_ref[...] = (g_ref[:, :d] * ua + g_ref[:, d:] * ub).astype(BF16)


def _merge(oa, ob, gate, wa, wb, *, tm=256):
    m, wdt = oa.shape
    d = wa.shape[1]
    assert m % tm == 0
    return pl.pallas_call(
        functools.partial(_merge_kernel, d=d),
        grid=(m // tm,),
        in_specs=[
            pl.BlockSpec((tm, wdt), lambda i: (i, 0)),
            pl.BlockSpec((tm, wdt), lambda i: (i, 0)),
            pl.BlockSpec((tm, 2 * d), lambda i: (i, 0)),
            pl.BlockSpec((wdt, d), lambda i: (0, 0)),
            pl.BlockSpec((wdt, d), lambda i: (0, 0)),
        ],
        out_specs=pl.BlockSpec((tm, d), lambda i: (i, 0)),
        out_shape=jax.ShapeDtypeStruct((m, d), BF16),
        compiler_params=_cparams(("parallel",)),
        name="merge",
    )(oa, ob, gate, wa, wb)


def _resid_mm_kernel(x_ref, m_ref, w_ref, o_ref):
    o_ref[...] = x_ref[...] + jnp.dot(m_ref[...], w_ref[...], preferred_element_type=F32)


def _resid_mm(x, mm, w, *, tm=256):
    m, d = x.shape
    k = mm.shape[1]
    assert m % tm == 0
    return pl.pallas_call(
        _resid_mm_kernel,
        grid=(m // tm,),
        in_specs=[
            pl.BlockSpec((tm, d), lambda i: (i, 0)),
            pl.BlockSpec((tm, k), lambda i: (i, 0)),
            pl.BlockSpec((k, d), lambda i: (0, 0)),
        ],
        out_specs=pl.BlockSpec((tm, d), lambda i: (i, 0)),
        out_shape=jax.ShapeDtypeStruct((m, d), F32),
        compiler_params=_cparams(("parallel",)),
        name="resid_mm",
    )(x, mm, w)


def _ple_kernel(x_ref, p_ref, h_ref, wp_ref, wg_ref, gn_ref, o_ref, y_ref):
    e = jnp.dot(p_ref[...], wp_ref[...], preferred_element_type=F32)
    gt = jnp.dot(h_ref[...], wg_ref[...], preferred_element_type=F32)
    x4 = x_ref[...] + e * _sigmoid(gt)
    o_ref[...] = x4
    y_ref[...] = _rms(x4, gn_ref[...])


def _ple(x, p, h, wp, wg, g_next, *, tm=256):
    m, d = x.shape
    pd = p.shape[1]
    assert m % tm == 0
    return pl.pallas_call(
        _ple_kernel,
        grid=(m // tm,),
        in_specs=[
            pl.BlockSpec((tm, d), lambda i: (i, 0)),
            pl.BlockSpec((tm, pd), lambda i: (i, 0)),
            pl.BlockSpec((tm, d), lambda i: (i, 0)),
            pl.BlockSpec((pd, d), lambda i: (0, 0)),
            pl.BlockSpec((d, d), lambda i: (0, 0)),
            pl.BlockSpec((1, d), lambda i: (0, 0)),
        ],
        out_specs=[pl.BlockSpec((tm, d), lambda i: (i, 0)), pl.BlockSpec((tm, d), lambda i: (i, 0))],
        out_shape=[jax.ShapeDtypeStruct((m, d), F32), jax.ShapeDtypeStruct((m, d), F32)],
        compiler_params=_cparams(("parallel",)),
        name="ple",
    )(x, p, h, wp, wg, g_next.reshape(1, d))


def _cumsum_kernel(lf_ref, o_ref, carry_sc, *, n_heads, tb):
    @pl.when(pl.program_id(0) == 0)
    def _():
        carry_sc[...] = jnp.zeros_like(carry_sc)

    row = lax.broadcasted_iota(I32, (tb, tb), 0)
    col = lax.broadcasted_iota(I32, (tb, tb), 1)
    tri = jnp.where(col <= row, 1.0, 0.0).astype(F32)
    lf = lf_ref[...]
    for h in range(n_heads):
        lfb = jnp.broadcast_to(lf[:, h:h + 1], (tb, LANES))
        cum = jnp.dot(tri, lfb, preferred_element_type=F32, precision=lax.Precision.HIGHEST)
        cum = cum + carry_sc[h:h + 1, :]
        o_ref[h] = cum
        carry_sc[h:h + 1, :] = cum[tb - 1:tb, :]


def _cumsum_rep(lf, *, tb=256):
    s, n_heads = lf.shape
    assert s % tb == 0
    return pl.pallas_call(
        functools.partial(_cumsum_kernel, n_heads=n_heads, tb=tb),
        grid=(s // tb,),
        in_specs=[pl.BlockSpec((tb, n_heads), lambda i: (i, 0))],
        out_specs=pl.BlockSpec((n_heads, tb, LANES), lambda i: (0, i, 0)),
        out_shape=jax.ShapeDtypeStruct((n_heads, s, LANES), F32),
        scratch_shapes=[pltpu.VMEM((n_heads, LANES), F32)],
        compiler_params=_cparams(("arbitrary",)),
        name="cumsum_logf",
    )(lf)


def _idx_prompt_kernel(ki_ref, qt_ref, w_ref, mask_ref, ks_ref, *, n_sel, tq, n_heads, cb):
    i = pl.program_id(0)
    n_blocks = pl.num_programs(0)
    big = cb * tq

    @pl.when(i == 0)
    def _():
        ks_ref[...] = jnp.full(ks_ref.shape, NEG_INF_KEY, I32)

    w = w_ref[0]
    row = lax.broadcasted_iota(I32, (tq, tq), 0)
    col = lax.broadcasted_iota(I32, (tq, tq), 1)
    hpd = 2

    def score_chunk(c, carry):
        r0 = pl.multiple_of(c * tq, tq)
        kc = ki_ref[pl.ds(r0, tq), :]
        acc = jnp.zeros((tq, tq), F32)
        for g in range(n_heads // hpd):
            sc = jnp.dot(kc, qt_ref[0, :, g * hpd * tq:(g + 1) * hpd * tq], preferred_element_type=F32)
            for u in range(hpd):
                h = g * hpd + u
                acc = acc + jnp.maximum(sc[:, u * tq:(u + 1) * tq], 0.0) * w[h:h + 1, :]
        acc = jnp.where(row + (c - i) * tq <= col, acc, NEG_INF)
        ks_ref[pl.ds(r0, tq), :] = _sort_key(acc)
        return carry

    lax.fori_loop(0, i + 1, score_chunk, 0)

    n_big = (i + cb) // cb

    def count_ge(cand):
        def body(c, acc):
            r0 = pl.multiple_of(c * big, big)
            hit = jnp.where(ks_ref[pl.ds(r0, big), :] >= cand, 1, 0).astype(I32)
            return acc + hit.reshape(big // SUBLANES, SUBLANES, tq).sum(axis=0)
        acc = lax.fori_loop(0, n_big, body, jnp.zeros((SUBLANES, tq), I32))
        return jnp.sum(acc, axis=0, keepdims=True)

    def bisect(b, ans_u):
        cand_u = ans_u | jnp.left_shift(jnp.int32(1), 31 - b)
        cnt = count_ge(cand_u ^ jnp.int32(INT_MIN))
        return jnp.where(cnt >= n_sel, cand_u, ans_u)

    ans_u = lax.fori_loop(0, 32, bisect, jnp.zeros((1, tq), I32))
    thr = ans_u ^ jnp.int32(INT_MIN)
    tied = jnp.max(jnp.abs(count_ge(thr) - n_sel)) > 0

    @pl.when(jnp.logical_not(tied))
    def _():
        def finish(c, carry):
            r0 = pl.multiple_of(c * tq, tq)
            key = ks_ref[pl.ds(r0, tq), :]
            mask_ref[pl.ds(r0, tq), :] = jnp.where(key >= thr, 0.0, NEG_INF).astype(BF16)
            return carry

        lax.fori_loop(0, i + 1, finish, 0)

    @pl.when(tied)
    def _():
        need = (n_sel - count_ge(thr + 1)).astype(F32)
        strict_lower = jnp.where(col < row, 1.0, 0.0).astype(BF16)

        def finish(c, seen):
            r0 = pl.multiple_of(c * tq, tq)
            key = ks_ref[pl.ds(r0, tq), :]
            eq = jnp.where(key == thr, 1.0, 0.0)
            before = jnp.dot(strict_lower, eq.astype(BF16), preferred_element_type=F32) + seen
            tie_ok = jnp.where(before < need, eq, 0.0)
            sel = jnp.where(key > thr, 1.0, tie_ok)
            mask_ref[pl.ds(r0, tq), :] = jnp.where(sel > 0.0, 0.0, NEG_INF).astype(BF16)
            return seen + jnp.sum(eq, axis=0, keepdims=True)

        lax.fori_loop(0, i + 1, finish, jnp.zeros((1, tq), F32))

    def fill(c, carry):
        r0 = pl.multiple_of(c * tq, tq)
        mask_ref[pl.ds(r0, tq), :] = jnp.full((tq, tq), NEG_INF, BF16)
        return carry

    lax.fori_loop(i + 1, n_blocks, fill, 0)


def _idx_prompt(ki, qt, w, *, n_sel, tq=128, cb=4):
    s, idim = ki.shape
    nb, n_heads, _ = w.shape
    assert nb * tq == s and (s // tq) % cb == 0 and cb * tq >= n_sel
    return pl.pallas_call(
        functools.partial(_idx_prompt_kernel, n_sel=n_sel, tq=tq, n_heads=n_heads, cb=cb),
        grid=(nb,),
        in_specs=[
            pl.BlockSpec((s, idim), lambda i: (0, 0)),
            pl.BlockSpec((1, idim, n_heads * tq), lambda i: (i, 0, 0)),
            pl.BlockSpec((1, n_heads, tq), lambda i: (i, 0, 0)),
        ],
        out_specs=pl.BlockSpec((s, tq), lambda i: (0, i)),
        out_shape=jax.ShapeDtypeStruct((s, s), BF16),
        scratch_shapes=[pltpu.VMEM((s, tq), I32)],
        compiler_params=_cparams(("arbitrary",)),
        name="idx_prompt",
    )(ki, qt, w)


def _attn_prompt_kernel(it_ref, jt_ref, qt_ref, k_ref, vt_ref, b1_ref, b2_ref, o_ref,
                        m_sc, l_sc, acc_sc, s_sc, *, mode, t, n_heads, hd, scale):
    s = pl.program_id(0)
    i = it_ref[s]
    j = jt_ref[s]

    @pl.when(j == 0)
    def _():
        m_sc[...] = jnp.full(m_sc.shape, M_INIT, F32)
        l_sc[...] = jnp.zeros_like(l_sc)
        acc_sc[...] = jnp.zeros_like(acc_sc)

    if mode == "fox":
        tile_bias = b2_ref[jnp.minimum(i - j, 1)]
    else:
        sel_bias = b1_ref[...].astype(F32)
        near = jnp.minimum(i - j, 2)

    tile_max = []
    for h in range(n_heads):
        hs = slice(h * hd, (h + 1) * hd)
        st = jnp.dot(k_ref[:, hs], qt_ref[hs, :], preferred_element_type=F32) * scale
        if mode == "fox":
            fk = b1_ref[h]
            st = st - jnp.concatenate([fk] * (t // LANES), axis=1) + tile_bias
        else:
            st = st + b2_ref[h, near] + sel_bias
        s_sc[h] = st
        tile_max.append(jnp.max(st, axis=0, keepdims=True))

    for h in range(n_heads):
        hs = slice(h * hd, (h + 1) * hd)
        st = s_sc[h]
        m_old = m_sc[h]
        m_new = jnp.maximum(m_old, tile_max[h])
        alpha = jnp.exp(m_old - m_new)
        p = jnp.exp(st - m_new)
        l_sc[h] = alpha * l_sc[h] + jnp.sum(p, axis=0, keepdims=True)
        pv = jnp.dot(vt_ref[hs, :], p.astype(BF16), preferred_element_type=F32)
        acc_sc[hs, :] = alpha * acc_sc[hs, :] + pv
        m_sc[h] = m_new

    @pl.when(j == i)
    def _():
        for h in range(n_heads):
            hs = slice(h * hd, (h + 1) * hd)
            o_ref[hs, :] = (acc_sc[hs, :] / l_sc[h]).astype(BF16)


def _attn_prompt(mode, qt, k, vt, b1, b2, *, t, n_heads, hd):
    width, s = qt.shape
    nb = s // t
    assert nb * t == s and t % LANES == 0
    pairs = [(i, j) for i in range(nb) for j in range(i + 1)]
    it = jnp.asarray([p[0] for p in pairs], I32)
    jt = jnp.asarray([p[1] for p in pairs], I32)
    if mode == "fox":
        b1_spec = pl.BlockSpec((n_heads, t, LANES), lambda g, it, jt: (0, jt[g], 0))
    else:
        b1_spec = pl.BlockSpec((t, t), lambda g, it, jt: (jt[g], it[g]))
    grid_spec = pltpu.PrefetchScalarGridSpec(
        num_scalar_prefetch=2,
        grid=(len(pairs),),
        in_specs=[
            pl.BlockSpec((width, t), lambda g, it, jt: (0, it[g])),
            pl.BlockSpec((t, width), lambda g, it, jt: (jt[g], 0)),
            pl.BlockSpec((width, t), lambda g, it, jt: (0, jt[g])),
            b1_spec,
            pl.BlockSpec(memory_space=pltpu.VMEM),
        ],
        out_specs=pl.BlockSpec((width, t), lambda g, it, jt: (0, it[g])),
        scratch_shapes=[
            pltpu.VMEM((n_heads, 1, t), F32),
            pltpu.VMEM((n_heads, 1, t), F32),
            pltpu.VMEM((width, t), F32),
            pltpu.VMEM((n_heads, t, t), F32),
        ],
    )
    return pl.pallas_call(
        functools.partial(_attn_prompt_kernel, mode=mode, t=t, n_heads=n_heads, hd=hd,
                          scale=hd ** -0.5),
        grid_spec=grid_spec,
        out_shape=jax.ShapeDtypeStruct((width, s), BF16),
        compiler_params=_cparams(("arbitrary",)),
        name="attn_prompt_" + mode,
    )(it, jt, qt, k, vt, b1, b2)


def _idx_sample_kernel(pt_ref, *refs, n_sel, pps, page, n_steps, t_new, n_heads, group):
    page_refs = refs[:pps]
    qi_ref, w_ref, kn_ref, mask_ref, sc_sc = refs[pps:]
    st = pl.program_id(1)
    span = pps * page
    width = sc_sc.shape[1]
    n_past = n_steps * span
    qi = qi_ref[0]
    w = w_ref[0]

    def scores(kb):
        sc = lax.dot_general(qi, kb, (((1,), (1,)), ((), ())), preferred_element_type=F32)
        sc = jnp.maximum(sc, 0.0) * jnp.concatenate([w] * (kb.shape[0] // LANES), axis=1)
        acc = sc[0:SUBLANES, :]
        for h in range(1, n_heads):
            acc = acc + sc[h * SUBLANES:(h + 1) * SUBLANES, :]
        return acc

    slot = pl.program_id(0) % group
    r0 = pl.multiple_of(slot * SUBLANES, SUBLANES)
    kb = jnp.concatenate([r[0].astype(BF16) for r in page_refs], axis=0)
    c0 = pl.multiple_of(st * span, span)
    sc_sc[pl.ds(r0, SUBLANES), pl.ds(c0, span)] = _sort_key(scores(kb))

    @pl.when(st == n_steps - 1)
    def _():
        qrow = lax.broadcasted_iota(I32, (SUBLANES, page), 0)
        jcol = lax.broadcasted_iota(I32, (SUBLANES, page), 1)
        new = jnp.where((jcol <= qrow) & (jcol < t_new), scores(kn_ref[0]), NEG_INF)
        sc_sc[pl.ds(r0, SUBLANES), n_past:n_past + page] = _sort_key(new)

    @pl.when((st == n_steps - 1) & (slot == group - 1))
    def _():
        rows = group * SUBLANES

        def bisect(b, ans_u):
            cand_u = ans_u | jnp.left_shift(jnp.int32(1), 31 - b)
            cand = cand_u ^ jnp.int32(INT_MIN)
            cnt = jnp.sum(jnp.where(sc_sc[...] >= cand, 1, 0).astype(I32), axis=1, keepdims=True)
            return jnp.where(cnt >= n_sel, cand_u, ans_u)

        ans_u = lax.fori_loop(0, 32, bisect, jnp.zeros((rows, 1), I32))
        thr = ans_u ^ jnp.int32(INT_MIN)
        n_gt = jnp.sum(jnp.where(sc_sc[...] > thr, 1, 0).astype(I32), axis=1, keepdims=True)
        need = (n_sel - n_gt).astype(F32)

        r = lax.broadcasted_iota(I32, (LANES, LANES), 0)
        c = lax.broadcasted_iota(I32, (LANES, LANES), 1)
        strict_upper = jnp.where(r < c, 1.0, 0.0).astype(BF16)
        seen = jnp.zeros((rows, 1), F32)
        for blk in range(width // LANES):
            cols = slice(blk * LANES, (blk + 1) * LANES)
            kblk = sc_sc[:, cols]
            eq = jnp.where(kblk == thr, 1.0, 0.0)
            before = jnp.dot(eq.astype(BF16), strict_upper, preferred_element_type=F32) + seen
            tie_ok = jnp.where(before < need, eq, 0.0)
            sel = jnp.where(kblk > thr, 1.0, tie_ok)
            mask_ref[:, :, cols] = jnp.where(sel > 0.0, 0.0, NEG_INF).reshape(group, SUBLANES, LANES)
            seen = seen + jnp.sum(eq, axis=1, keepdims=True)


def _idx_sample(pt_flat, pool_ik, qi, w, kn, *, n_sel, n_pages, t_new, n_heads, pps=8, group=8):
    b = qi.shape[0]
    page, idim = pool_ik.shape[1:]
    pps = math.gcd(pps, n_pages)
    group = math.gcd(group, b)
    assert page == LANES
    n_steps = n_pages // pps
    width = n_pages * page + page

    def page_spec(r):
        return pl.BlockSpec((1, page, idim),
                            lambda bi, st, pt: (pt[bi * n_pages + st * pps + r], 0, 0))

    grid_spec = pltpu.PrefetchScalarGridSpec(
        num_scalar_prefetch=1,
        grid=(b, n_steps),
        in_specs=[page_spec(r) for r in range(pps)] + [
            pl.BlockSpec((1, n_heads * SUBLANES, idim), lambda bi, st, pt: (bi, 0, 0)),
            pl.BlockSpec((1, n_heads * SUBLANES, LANES), lambda bi, st, pt: (bi, 0, 0)),
            pl.BlockSpec((1, page, idim), lambda bi, st, pt: (bi, 0, 0)),
        ],
        out_specs=pl.BlockSpec((group, SUBLANES, width), lambda bi, st, pt: (bi // group, 0, 0)),
        scratch_shapes=[pltpu.VMEM((group * SUBLANES, width), I32)],
    )
    return pl.pallas_call(
        functools.partial(_idx_sample_kernel, n_sel=n_sel, pps=pps, page=page, n_steps=n_steps,
                          t_new=t_new, n_heads=n_heads, group=group),
        grid_spec=grid_spec,
        out_shape=jax.ShapeDtypeStruct((b, SUBLANES, width), F32),
        compiler_params=_cparams(("arbitrary", "arbitrary")),
        name="idx_sample",
    )(pt_flat, *([pool_ik] * pps), qi, w, kn)


def _attn_sample_kernel(pt_ref, *refs, mode, pps, page, n_steps, t_new, n_heads, hd, scale):
    k_refs = refs[:pps]
    v_refs = refs[pps:2 * pps]
    rest = refs[2 * pps:]
    if mode == "fox":
        lf_refs = rest[:pps]
        q_ref, kn_ref, vn_ref, lfn_ref, o_ref, m_sc, l_sc, acc_sc, carry_sc = rest[pps:]
    else:
        q_ref, kn_ref, vn_ref, mask_ref, t5_ref, o_ref, m_sc, l_sc, acc_sc = rest
    st = pl.program_id(1)
    rows = t_new * n_heads
    n_past = n_steps * pps * page

    @pl.when(st == 0)
    def _():
        m_sc[...] = jnp.full(m_sc.shape, M_INIT, F32)
        l_sc[...] = jnp.zeros_like(l_sc)
        acc_sc[...] = jnp.zeros_like(acc_sc)
        if mode == "fox":
            carry_sc[...] = jnp.zeros_like(carry_sc)

    span = pps * page
    is_last = st == n_steps - 1

    kb = jnp.concatenate([r[0].astype(BF16) for r in k_refs] + [kn_ref[0]], axis=0)
    vb = jnp.concatenate([r[0].astype(BF16) for r in v_refs] + [vn_ref[0]], axis=0)

    if mode == "fox":
        ru = lax.broadcasted_iota(I32, (page, page), 0)
        cu = lax.broadcasted_iota(I32, (page, page), 1)
        lower_incl = jnp.where(cu <= ru, 1.0, 0.0).astype(BF16)
        blocks = []
        before = jnp.zeros((1, 3 * n_heads), F32)
        for r in list(lf_refs) + [lfn_ref]:
            c = jnp.dot(lower_incl, jnp.concatenate(_split3(r[0]), axis=1), preferred_element_type=F32)
            blocks.append(c + before)
            before = before + c[page - 1:page, :]
        cum_parts = jnp.concatenate(_split3(jnp.concatenate(blocks, axis=0)), axis=1)
        rh = lax.broadcasted_iota(I32, (rows, 9 * n_heads), 0) % n_heads
        ch = lax.broadcasted_iota(I32, (rows, 9 * n_heads), 1) % n_heads
        head_of_row = jnp.where(rh == ch, 1.0, 0.0).astype(BF16)
        cum_t = lax.dot_general(head_of_row, cum_parts, (((1,), (1,)), ((), ())),
                                preferred_element_type=F32)
        f_cum = cum_t + carry_sc[...]
        carry_sc[...] = f_cum[:, span - 1:span]
        qrow = lax.broadcasted_iota(I32, (rows, page), 0) // n_heads
        jcol = lax.broadcasted_iota(I32, (rows, page), 1)
        new_ok = (jcol <= qrow) & (jcol < t_new) & is_last
        bias = jnp.concatenate([-f_cum[:, :span], jnp.where(new_ok, -f_cum[:, span:], NEG_INF)], axis=1)
    else:
        def mask_rows(mk):
            return jnp.concatenate(
                [jnp.broadcast_to(mk[qq:qq + 1, :], (n_heads, mk.shape[1])) for qq in range(t_new)], axis=0)

        c0 = pl.multiple_of(st * span, span)
        past_bias = t5_ref[:, pl.ds(c0, span)] + mask_rows(mask_ref[0, :, pl.ds(c0, span)])
        new_bias = t5_ref[:, n_past:n_past + page] + mask_rows(mask_ref[0, :, n_past:n_past + page])
        bias = jnp.concatenate([past_bias, jnp.where(is_last, new_bias, NEG_INF)], axis=1)

    s = lax.dot_general(q_ref[0], kb, (((1,), (1,)), ((), ())), preferred_element_type=F32)
    s = s * scale + bias
    m_old = m_sc[...]
    m_new = jnp.maximum(m_old, jnp.max(s, axis=1, keepdims=True))
    alpha = jnp.exp(m_old - m_new)
    p = jnp.exp(s - m_new)
    l_sc[...] = alpha * l_sc[...] + jnp.sum(p, axis=1, keepdims=True)
    acc_sc[...] = alpha * acc_sc[...] + jnp.dot(p.astype(BF16), vb, preferred_element_type=F32)
    m_sc[...] = m_new

    @pl.when(is_last)
    def _():
        acc = acc_sc[...]
        hrow = lax.broadcasted_iota(I32, (rows, hd), 0) % n_heads
        out = jnp.zeros((rows, hd), F32)
        for h in range(n_heads):
            out = out + jnp.where(hrow == h, acc[:, h * hd:(h + 1) * hd], 0.0)
        o_ref[0] = (out / l_sc[...]).astype(BF16)


def _attn_sample(mode, pt_flat, pool_k, pool_v, qbd, kn, vn, extra, *, n_pages, t_new, n_heads, hd, pps=4):
    b, rows, width = qbd.shape
    page = pool_k.shape[1]
    assert n_pages % pps == 0 and page == LANES and rows == t_new * n_heads
    n_steps = n_pages // pps

    def page_spec(shape_tail, r):
        zeros = (0,) * len(shape_tail)
        return pl.BlockSpec((1,) + shape_tail,
                            lambda bi, st, pt: (pt[bi * n_pages + st * pps + r],) + zeros)

    def per_req(shape_tail):
        zeros = (0,) * len(shape_tail)
        return pl.BlockSpec((1,) + shape_tail, lambda bi, st, pt: (bi,) + zeros)

    in_specs = [page_spec((page, width), r) for r in range(pps)] * 2
    args = [pool_k] * pps + [pool_v] * pps
    scratch = [pltpu.VMEM((rows, 1), F32), pltpu.VMEM((rows, 1), F32), pltpu.VMEM((rows, width), F32)]
    if mode == "fox":
        pool_lf, lfn = extra
        in_specs += [page_spec((page, n_heads), r) for r in range(pps)]
        args += [pool_lf] * pps
        in_specs += [per_req((rows, width)), per_req((page, width)), per_req((page, width)),
                     per_req((page, n_heads))]
        args += [qbd, kn, vn, lfn]
        scratch.append(pltpu.VMEM((rows, 1), F32))
    else:
        mask, t5 = extra
        in_specs += [per_req((rows, width)), per_req((page, width)), per_req((page, width)),
                     per_req(mask.shape[1:]), pl.BlockSpec(memory_space=pltpu.VMEM)]
        args += [qbd, kn, vn, mask, t5]
    grid_spec = pltpu.PrefetchScalarGridSpec(
        num_scalar_prefetch=1,
        grid=(b, n_steps),
        in_specs=in_specs,
        out_specs=pl.BlockSpec((1, rows, hd), lambda bi, st, pt: (bi, 0, 0)),
        scratch_shapes=scratch,
    )
    return pl.pallas_call(
        functools.partial(_attn_sample_kernel, mode=mode, pps=pps, page=page, n_steps=n_steps,
                          t_new=t_new, n_heads=n_heads, hd=hd, scale=hd ** -0.5),
        grid_spec=grid_spec,
        out_shape=jax.ShapeDtypeStruct((b, rows, hd), BF16),
        compiler_params=_cparams(("arbitrary", "arbitrary")),
        name="attn_sample_" + mode,
    )(pt_flat, *args)


def _t5_bucket(dist, num_buckets):
    n = jnp.maximum(dist, 0)
    exact = num_buckets // 2
    nf = jnp.maximum(n, 1).astype(F32)
    large = exact + (jnp.log(nf / exact) / math.log(MAX_DISTANCE / exact)
                     * (num_buckets - exact)).astype(I32)
    return jnp.where(n < exact, n, jnp.minimum(large, num_buckets - 1))


def _t5_bias(rel_bias, dist):
    nb = rel_bias.shape[0]
    onehot = (_t5_bucket(dist, nb)[..., None] == jnp.arange(nb)).astype(F32)
    b = jnp.einsum("...n,nh->h...", onehot, rel_bias.astype(F32), precision=lax.Precision.HIGHEST)
    return jnp.where(dist[None] >= 0, b, NEG_INF)


def _is_pow2(x):
    return math.frexp(x)[0] == 0.5


def _mix_prompt(q_a, k_a, v_a, lf_a, q_b, k_b, v_b, q_i, k_i, w_i, rel_bias, *, n_heads, hd, t=256):
    s = q_a.shape[0]
    idim = k_i.shape[1]
    ih = w_i.shape[1]
    n_sel = min(TOPK_MAX, s // 4)
    assert MAX_DISTANCE <= t + 1
    idx_scale, idx_w_scale = idim ** -0.5, ih ** -0.5
    assert _is_pow2(idx_scale) and _is_pow2(idx_w_scale)

    fk = _cumsum_rep(lf_a)
    r = jnp.arange(t)
    dist0 = r[None, :] - r[:, None]
    causal = jnp.where(dist0 >= 0, 0.0, NEG_INF).astype(F32)
    ctiles = jnp.stack([causal, jnp.zeros_like(causal)])
    o_a = _attn_prompt("fox", q_a.T, k_a, v_a.T, fk, ctiles, t=t, n_heads=n_heads, hd=hd).T

    tq = LANES
    nb = s // tq
    qt = q_i.reshape(nb, tq, ih, idim).transpose(0, 3, 2, 1).reshape(nb, idim, ih * tq)
    w = (w_i * (idx_scale * idx_w_scale)).reshape(nb, tq, ih).transpose(0, 2, 1)
    mask = _idx_prompt(k_i, qt, w, n_sel=n_sel, tq=tq)
    btiles = jnp.stack([_t5_bias(rel_bias, dist0 + nd * t) for nd in range(3)], axis=1)
    o_b = _attn_prompt("dsa", q_b.T, k_b, v_b.T, mask, btiles, t=t, n_heads=n_heads, hd=hd).T
    return o_a, o_b


def _block_diag_q(q, n_heads, hd):
    b, t_new, _ = q.shape
    qh = q.reshape(b, t_new, n_heads, hd)
    eye = jnp.eye(n_heads, dtype=q.dtype)
    return jnp.einsum("bthd,hg->bthgd", qh, eye).reshape(b, t_new * n_heads, n_heads * hd)


def _pad_rows(x, rows):
    return jnp.pad(x, ((0, 0), (0, rows - x.shape[1]), (0, 0)))


def _mix_sample(q_a, k_a, v_a, lf_a, q_b, k_b, v_b, q_i, k_i, w_i, rel_bias, page_table,
                fox_k, fox_v, fox_lf, dsa_k, dsa_v, idx_k, *, layer, n_heads, hd):
    b, t_new, width = q_a.shape
    n_pages = page_table.shape[1]
    depth, n_pool, page = fox_k.shape[:3]
    idim = k_i.shape[2]
    ih = w_i.shape[2]
    past = n_pages * page
    n_sel = min(TOPK_MAX, (past + t_new) // 4)
    idx_scale, idx_w_scale = idim ** -0.5, ih ** -0.5
    assert _is_pow2(idx_scale) and _is_pow2(idx_w_scale)
    assert t_new <= SUBLANES
    pt_flat = page_table.reshape(-1).astype(I32) + layer * n_pool
    n_pool = depth * n_pool
    fox_lf = fox_lf.reshape(n_pool, page, n_heads)
    idx_k = idx_k.reshape(n_pool, page, idim)

    kn_a, vn_a = _pad_rows(k_a, page), _pad_rows(v_a, page)
    lfn = _pad_rows(lf_a, page)
    o_a = _attn_sample("fox", pt_flat, fox_k.reshape(n_pool, page, width), fox_v.reshape(n_pool, page, width),
                       _block_diag_q(q_a, n_heads, hd), kn_a, vn_a, (fox_lf, lfn),
                       n_pages=n_pages, t_new=t_new, n_heads=n_heads, hd=hd)

    qi = jnp.pad(q_i.reshape(b, t_new, ih, idim), ((0, 0), (0, SUBLANES - t_new), (0, 0), (0, 0)))
    qi = qi.transpose(0, 2, 1, 3).reshape(b, ih * SUBLANES, idim)
    wr = jnp.pad(w_i * (idx_scale * idx_w_scale), ((0, 0), (0, SUBLANES - t_new), (0, 0)))
    wr = jnp.broadcast_to(wr.transpose(0, 2, 1).reshape(b, ih * SUBLANES, 1), (b, ih * SUBLANES, LANES))
    mask = _idx_sample(pt_flat, idx_k, qi, wr, _pad_rows(k_i, page), n_sel=n_sel, n_pages=n_pages,
                       t_new=t_new, n_heads=ih)
    kpos = jnp.arange(past + page)
    qpos = past + jnp.arange(t_new)
    dist = jnp.where(kpos[None, :] < past + t_new, qpos[:, None] - kpos[None, :], -1)
    t5 = _t5_bias(rel_bias, dist)
    t5 = t5.transpose(1, 0, 2).reshape(t_new * n_heads, past + page)
    o_b = _attn_sample("dsa", pt_flat, dsa_k.reshape(n_pool, page, width), dsa_v.reshape(n_pool, page, width),
                       _block_diag_q(q_b, n_heads, hd), _pad_rows(k_b, page), _pad_rows(v_b, page),
                       (mask, t5), n_pages=n_pages, t_new=t_new, n_heads=n_heads, hd=hd)
    return o_a.reshape(b * t_new, width), o_b.reshape(b * t_new, width)


def kernel(x_prompt, x_sample, cache_fox_k, cache_fox_v, cache_fox_logf, cache_dsa_k, cache_dsa_v, cache_idx_k, page_table, p_prompt, p_sample, rel_bias, g_ffn1, w1_pre, w3_pre, w2_pre, g_mix, w_in, b_forget, w_branch_fox, w_branch_dsa, w_out, g_ffn2, w1_post, w3_post, w2_post, g_ple, w_ple, w_ple_gate, g_final):
    depth = w_in.shape[0]
    bp, sp, d = x_prompt.shape
    bs, ts, _ = x_sample.shape
    assert bp == 1
    n_fox, hd = cache_fox_k.shape[3:]
    n_dsa = cache_dsa_k.shape[3]
    assert n_fox == n_dsa
    idim = cache_idx_k.shape[-1]
    ih = IDX_HEADS
    fw, dw = n_fox * hd, n_dsa * hd
    n_in = w_in.shape[2]
    assert n_in == 3 * fw + n_fox + 3 * dw + ih * idim + idim + ih + 2 * d
    mp, ms = bp * sp, bs * ts
    small_w = n_fox + idim + ih
    assert small_w <= LANES

    x = jnp.concatenate([x_prompt.reshape(mp, d), x_sample.reshape(ms, d)], axis=0)
    st_p, st_s = [], []
    for l in range(depth):
        offs = np.cumsum([0, fw, fw, fw, n_fox, dw, dw, dw, ih * idim, idim, ih, 2 * d])
        col = lambda a, c: w_in[l][:, offs[a]:offs[a] + c]
        w_qkv = jnp.concatenate([col(0, fw), col(4, dw), col(7, ih * idim),
                                 col(1, fw), col(2, fw), col(5, dw), col(6, dw)], axis=1).astype(BF16)
        w_gate = col(10, 2 * d).astype(BF16)
        w_small = jnp.concatenate([col(3, n_fox), col(8, idim), col(9, ih)], axis=1)
        w_small = jnp.pad(w_small, ((0, 0), (0, LANES - small_w))).astype(BF16)
        b_small = jnp.pad(b_forget[l], (0, LANES - n_fox)).reshape(1, LANES)

        x1, h_mix = _ffn(x, g_ffn1[l], w1_pre[l].astype(BF16), w3_pre[l].astype(BF16),
                         w2_pre[l].astype(BF16), g_mix[l])
        z32, z16 = _proj("qkv", h_mix, w_qkv)
        gate = _proj("gate", h_mix, w_gate)
        small = _proj("small", h_mix, w_small, bias=b_small, n_fox=n_fox)

        q_a, q_b, q_i = z16[:, :fw], z16[:, fw:fw + dw], z16[:, fw + dw:fw + dw + ih * idim]
        kv0 = fw + dw + ih * idim
        kv16 = [z16[:, kv0 + n * fw:kv0 + (n + 1) * fw] for n in range(4)]
        kv32 = [z32[:, kv0 + n * fw:kv0 + (n + 1) * fw] for n in range(4)]
        lf = small[:, :n_fox]
        k_i32 = small[:, n_fox:n_fox + idim]
        w_i = small[:, n_fox + idim:small_w]
        k_i16 = k_i32.astype(BF16)

        o_ap, o_bp = _mix_prompt(q_a[:mp], kv16[0][:mp], kv16[1][:mp], lf[:mp], q_b[:mp], kv16[2][:mp],
                                 kv16[3][:mp], q_i[:mp], k_i16[:mp], w_i[:mp], rel_bias,
                                 n_heads=n_fox, hd=hd)
        rs = lambda a: a[mp:].reshape(bs, ts, -1)
        o_as, o_bs = _mix_sample(rs(q_a), rs(kv16[0]), rs(kv16[1]), rs(lf), rs(q_b), rs(kv16[2]),
                                 rs(kv16[3]), rs(q_i), rs(k_i16), rs(w_i), rel_bias, page_table,
                                 cache_fox_k, cache_fox_v, cache_fox_logf, cache_dsa_k,
                                 cache_dsa_v, cache_idx_k, layer=l, n_heads=n_fox, hd=hd)
        o_a = jnp.concatenate([o_ap, o_as], axis=0)
        o_b = jnp.concatenate([o_bp, o_bs], axis=0)

        merged = _merge(o_a, o_b, gate, w_branch_fox[l].astype(BF16), w_branch_dsa[l].astype(BF16))
        x2 = _resid_mm(x1, merged, w_out[l].astype(BF16))
        x3, h_ple = _ffn(x2, g_ffn2[l], w1_post[l].astype(BF16), w3_post[l].astype(BF16),
                         w2_post[l].astype(BF16), g_ple[l])
        p = jnp.concatenate([p_prompt[l].reshape(mp, -1), p_sample[l].reshape(ms, -1)], axis=0).astype(BF16)
        x, y = _ple(x3, p, h_ple, w_ple[l].astype(BF16), w_ple_gate[l].astype(BF16), g_final)

        heads = lambda a, bb, tt: a.reshape(bb, tt, n_fox, hd)
        st_p.append((heads(kv32[0][:mp], bp, sp), heads(kv32[1][:mp], bp, sp), lf[:mp].reshape(bp, sp, n_fox),
                     heads(kv32[2][:mp], bp, sp), heads(kv32[3][:mp], bp, sp), k_i32[:mp].reshape(bp, sp, idim)))
        st_s.append((heads(kv32[0][mp:], bs, ts), heads(kv32[1][mp:], bs, ts), lf[mp:].reshape(bs, ts, n_fox),
                     heads(kv32[2][mp:], bs, ts), heads(kv32[3][mp:], bs, ts), k_i32[mp:].reshape(bs, ts, idim)))

    pfk, pfv, pflf, pdk, pdv, pik = [jnp.stack(a) for a in zip(*st_p)]
    sfk, sfv, sflf, sdk, sdv, sik = [jnp.stack(a) for a in zip(*st_s)]
    y_prompt = y[:mp].reshape(bp, sp, d)
    y_sample = y[mp:].reshape(bs, ts, d)
    return (y_prompt, y_sample, pfk, pfv, pflf, pdk, pdv, pik, sfk, sfv, sflf, sdk, sdv, sik)
```

```python
import functools
import math

import numpy as np
import jax
import jax.numpy as jnp
from jax import lax
from jax.experimental import pallas as pl
from jax.experimental.pallas import tpu as pltpu

F32 = jnp.float32
BF16 = jnp.bfloat16
I32 = jnp.int32

TOPK_MAX = 256
MAX_DISTANCE = 128
RMS_EPS = 1e-6
IDX_HEADS = 16

V7X_VMEM_BYTES = 64 * 1024 * 1024
VMEM_LIMIT = V7X_VMEM_BYTES - 8 * 1024 * 1024
LANES = 128
SUBLANES = 8

LOG2E = math.log2(math.e)
NEG_INF = float("-inf")
M_INIT = -1e30
INT_MIN = -2 ** 31
NEG_INF_KEY = int(np.int32(np.uint32(0xFF800000) ^ np.uint32(0x7FFFFFFF)))


def _cparams(sem):
    return pltpu.CompilerParams(dimension_semantics=sem, vmem_limit_bytes=VMEM_LIMIT)


def _rms(x, g):
    ms = jnp.mean(x * x, axis=-1, keepdims=True)
    return x * lax.rsqrt(ms + RMS_EPS) * g


def _sigmoid(x):
    return 1.0 / (1.0 + jnp.exp(-x))


def _split3(x):
    hi = x.astype(BF16)
    r1 = x - hi.astype(F32)
    mid = r1.astype(BF16)
    lo = (r1 - mid.astype(F32)).astype(BF16)
    return hi, mid, lo


def _sort_key(x):
    bits = pltpu.bitcast(x, I32)
    return bits ^ ((bits >> 31) & jnp.int32(0x7FFFFFFF))


def _ffn_kernel(x_ref, g_ref, w1_ref, w3_ref, w2_ref, gn_ref, o_ref, hn_ref, h_sc, acc_sc):
    f = pl.program_id(1)

    @pl.when(f == 0)
    def _():
        h_sc[...] = _rms(x_ref[...], g_ref[...]).astype(BF16)
        acc_sc[...] = jnp.zeros_like(acc_sc)

    h = h_sc[...]
    a = jnp.dot(h, w1_ref[...], preferred_element_type=F32)
    b = jnp.dot(h, w3_ref[...], preferred_element_type=F32)
    act = (a * _sigmoid(a) * b).astype(BF16)
    acc_sc[...] += jnp.dot(act, w2_ref[...], preferred_element_type=F32)

    @pl.when(f == pl.num_programs(1) - 1)
    def _():
        x1 = x_ref[...] + 0.5 * acc_sc[...]
        o_ref[...] = x1
        hn_ref[...] = _rms(x1, gn_ref[...]).astype(BF16)


def _ffn(x, g, w1, w3, w2, g_next, *, tm=512, tf=512):
    m, d = x.shape
    dff = w1.shape[1]
    assert m % tm == 0 and dff % tf == 0
    return pl.pallas_call(
        _ffn_kernel,
        grid=(m // tm, dff // tf),
        in_specs=[
            pl.BlockSpec((tm, d), lambda i, f: (i, 0)),
            pl.BlockSpec((1, d), lambda i, f: (0, 0)),
            pl.BlockSpec((d, tf), lambda i, f: (0, f)),
            pl.BlockSpec((d, tf), lambda i, f: (0, f)),
            pl.BlockSpec((tf, d), lambda i, f: (f, 0)),
            pl.BlockSpec((1, d), lambda i, f: (0, 0)),
        ],
        out_specs=[
            pl.BlockSpec((tm, d), lambda i, f: (i, 0)),
            pl.BlockSpec((tm, d), lambda i, f: (i, 0)),
        ],
        out_shape=[jax.ShapeDtypeStruct((m, d), F32), jax.ShapeDtypeStruct((m, d), BF16)],
        scratch_shapes=[pltpu.VMEM((tm, d), BF16), pltpu.VMEM((tm, d), F32)],
        compiler_params=_cparams(("parallel", "arbitrary")),
        name="ffn",
    )(x, g.reshape(1, d), w1, w3, w2, g_next.reshape(1, d))


def _proj_qkv_kernel(h_ref, w_ref, o32_ref, o16_ref):
    z = jnp.dot(h_ref[...], w_ref[...], preferred_element_type=F32)
    o32_ref[...] = z
    o16_ref[...] = z.astype(BF16)


def _proj_gate_kernel(h_ref, w_ref, o_ref):
    z = jnp.dot(h_ref[...], w_ref[...], preferred_element_type=F32)
    o_ref[...] = _sigmoid(z)


def _proj_small_kernel(h_ref, w_ref, b_ref, o_ref, *, n_fox):
    z = jnp.dot(h_ref[...], w_ref[...], preferred_element_type=F32)
    lane = lax.broadcasted_iota(I32, z.shape, 1)
    zf = z + b_ref[...]
    logsig = jnp.minimum(zf, 0.0) - jnp.log1p(jnp.exp(-jnp.abs(zf)))
    o_ref[...] = jnp.where(lane < n_fox, logsig, z)


def _proj(kind, h, w, *, tm=512, tn=1024, bias=None, n_fox=0):
    m, k = h.shape
    n = w.shape[1]
    tn = min(tn, n)
    assert m % tm == 0 and n % tn == 0
    grid = (n // tn, m // tm)
    h_spec = pl.BlockSpec((tm, k), lambda j, i: (i, 0))
    w_spec = pl.BlockSpec((k, tn), lambda j, i: (0, j))
    o_spec = pl.BlockSpec((tm, tn), lambda j, i: (i, j))
    cp = _cparams(("parallel", "parallel"))
    if kind == "qkv":
        return pl.pallas_call(
            _proj_qkv_kernel, grid=grid, in_specs=[h_spec, w_spec], out_specs=[o_spec, o_spec],
            out_shape=[jax.ShapeDtypeStruct((m, n), F32), jax.ShapeDtypeStruct((m, n), BF16)],
            compiler_params=cp, name="proj_qkv")(h, w)
    if kind == "gate":
        return pl.pallas_call(
            _proj_gate_kernel, grid=grid, in_specs=[h_spec, w_spec], out_specs=o_spec,
            out_shape=jax.ShapeDtypeStruct((m, n), F32), compiler_params=cp, name="proj_gate")(h, w)
    assert kind == "small"
    b_spec = pl.BlockSpec((1, tn), lambda j, i: (0, j))
    return pl.pallas_call(
        functools.partial(_proj_small_kernel, n_fox=n_fox), grid=grid,
        in_specs=[h_spec, w_spec, b_spec], out_specs=o_spec,
        out_shape=jax.ShapeDtypeStruct((m, n), F32), compiler_params=cp, name="proj_small")(h, w, bias)


def _merge_kernel(oa_ref, ob_ref, g_ref, wa_ref, wb_ref, m_ref, *, d):
    ua = jnp.dot(oa_ref[...], wa_ref[...], preferred_element_type=F32)
    ub = jnp.dot(ob_ref[...], wb_ref[...], preferred_element_type=F32)
    m_ref[...] = (g_ref[:, :d] * ua + g_ref[:, d:] * ub).astype(BF16)


def _merge(oa, ob, gate, wa, wb, *, tm=256):
    m, wdt = oa.shape
    d = wa.shape[1]
    assert m % tm == 0
    return pl.pallas_call(
        functools.partial(_merge_kernel, d=d),
        grid=(m // tm,),
        in_specs=[
            pl.BlockSpec((tm, wdt), lambda i: (i, 0)),
            pl.BlockSpec((tm, wdt), lambda i: (i, 0)),
            pl.BlockSpec((tm, 2 * d), lambda i: (i, 0)),
            pl.BlockSpec((wdt, d), lambda i: (0, 0)),
            pl.BlockSpec((wdt, d), lambda i: (0, 0)),
        ],
        out_specs=pl.BlockSpec((tm, d), lambda i: (i, 0)),
        out_shape=jax.ShapeDtypeStruct((m, d), BF16),
        compiler_params=_cparams(("parallel",)),
        name="merge",
    )(oa, ob, gate, wa, wb)


def _resid_mm_kernel(x_ref, m_ref, w_ref, o_ref):
    o_ref[...] = x_ref[...] + jnp.dot(m_ref[...], w_ref[...], preferred_element_type=F32)


def _resid_mm(x, mm, w, *, tm=256):
    m, d = x.shape
    k = mm.shape[1]
    assert m % tm == 0
    return pl.pallas_call(
        _resid_mm_kernel,
        grid=(m // tm,),
        in_specs=[
            pl.BlockSpec((tm, d), lambda i: (i, 0)),
            pl.BlockSpec((tm, k), lambda i: (i, 0)),
            pl.BlockSpec((k, d), lambda i: (0, 0)),
        ],
        out_specs=pl.BlockSpec((tm, d), lambda i: (i, 0)),
        out_shape=jax.ShapeDtypeStruct((m, d), F32),
        compiler_params=_cparams(("parallel",)),
        name="resid_mm",
    )(x, mm, w)


def _ple_kernel(x_ref, p_ref, h_ref, wp_ref, wg_ref, gn_ref, o_ref, y_ref):
    e = jnp.dot(p_ref[...], wp_ref[...], preferred_element_type=F32)
    gt = jnp.dot(h_ref[...], wg_ref[...], preferred_element_type=F32)
    x4 = x_ref[...] + e * _sigmoid(gt)
    o_ref[...] = x4
    y_ref[...] = _rms(x4, gn_ref[...])


def _ple(x, p, h, wp, wg, g_next, *, tm=256):
    m, d = x.shape
    pd = p.shape[1]
    assert m % tm == 0
    return pl.pallas_call(
        _ple_kernel,
        grid=(m // tm,),
        in_specs=[
            pl.BlockSpec((tm, d), lambda i: (i, 0)),
            pl.BlockSpec((tm, pd), lambda i: (i, 0)),
            pl.BlockSpec((tm, d), lambda i: (i, 0)),
            pl.BlockSpec((pd, d), lambda i: (0, 0)),
            pl.BlockSpec((d, d), lambda i: (0, 0)),
            pl.BlockSpec((1, d), lambda i: (0, 0)),
        ],
        out_specs=[pl.BlockSpec((tm, d), lambda i: (i, 0)), pl.BlockSpec((tm, d), lambda i: (i, 0))],
        out_shape=[jax.ShapeDtypeStruct((m, d), F32), jax.ShapeDtypeStruct((m, d), F32)],
        compiler_params=_cparams(("parallel",)),
        name="ple",
    )(x, p, h, wp, wg, g_next.reshape(1, d))


def _cumsum_kernel(lf_ref, o_ref, carry_sc, *, n_heads, tb, out_scale):
    @pl.when(pl.program_id(0) == 0)
    def _():
        carry_sc[...] = jnp.zeros_like(carry_sc)

    row = lax.broadcasted_iota(I32, (tb, tb), 0)
    col = lax.broadcasted_iota(I32, (tb, tb), 1)
    tri = jnp.where(col <= row, 1.0, 0.0).astype(F32)
    lf = lf_ref[...]
    for h in range(n_heads):
        lfb = jnp.broadcast_to(lf[:, h:h + 1], (tb, LANES))
        cum = jnp.dot(tri, lfb, preferred_element_type=F32, precision=lax.Precision.HIGHEST)
        cum = cum + carry_sc[h:h + 1, :]
        o_ref[h] = cum * out_scale
        carry_sc[h:h + 1, :] = cum[tb - 1:tb, :]


def _cumsum_rep(lf, *, out_scale, tb=256):
    s, n_heads = lf.shape
    assert s % tb == 0
    return pl.pallas_call(
        functools.partial(_cumsum_kernel, n_heads=n_heads, tb=tb, out_scale=out_scale),
        grid=(s // tb,),
        in_specs=[pl.BlockSpec((tb, n_heads), lambda i: (i, 0))],
        out_specs=pl.BlockSpec((n_heads, tb, LANES), lambda i: (0, i, 0)),
        out_shape=jax.ShapeDtypeStruct((n_heads, s, LANES), F32),
        scratch_shapes=[pltpu.VMEM((n_heads, LANES), F32)],
        compiler_params=_cparams(("arbitrary",)),
        name="cumsum_logf",
    )(lf)


def _idx_prompt_kernel(ki_ref, qt_ref, w_ref, mask_ref, ks_ref, *, n_sel, tq, n_heads, cb):
    i = pl.program_id(0)
    n_blocks = pl.num_programs(0)
    big = cb * tq

    @pl.when(i == 0)
    def _():
        ks_ref[...] = jnp.full(ks_ref.shape, NEG_INF_KEY, I32)

    w = w_ref[0]
    row = lax.broadcasted_iota(I32, (tq, tq), 0)
    col = lax.broadcasted_iota(I32, (tq, tq), 1)
    hpd = 2

    def score_chunk(c):
        r0 = pl.multiple_of(c * tq, tq)
        kc = ki_ref[pl.ds(r0, tq), :]
        acc = jnp.zeros((tq, tq), F32)
        for g in range(n_heads // hpd):
            sc = jnp.dot(kc, qt_ref[0, :, g * hpd * tq:(g + 1) * hpd * tq], preferred_element_type=F32)
            for u in range(hpd):
                h = g * hpd + u
                acc = acc + jnp.maximum(sc[:, u * tq:(u + 1) * tq], 0.0) * w[h:h + 1, :]
        acc = jnp.where(row + (c - i) * tq <= col, acc, NEG_INF)
        ks_ref[pl.ds(r0, tq), :] = _sort_key(acc)

    def score_pair(pr, carry):
        score_chunk(2 * pr)
        score_chunk(2 * pr + 1)
        return carry

    lax.fori_loop(0, (i + 2) // 2, score_pair, 0)

    n_big = (i + cb) // cb

    def count_ge(cand):
        def body(c, acc):
            r0 = pl.multiple_of(c * big, big)
            hit = jnp.where(ks_ref[pl.ds(r0, big), :] >= cand, 1, 0).astype(I32)
            return acc + hit.reshape(big // SUBLANES, SUBLANES, tq).sum(axis=0)
        acc = lax.fori_loop(0, n_big, body, jnp.zeros((SUBLANES, tq), I32))
        return jnp.sum(acc, axis=0, keepdims=True)

    def bisect(b, ans_u):
        cand_u = ans_u | jnp.left_shift(jnp.int32(1), 31 - b)
        cnt = count_ge(cand_u ^ jnp.int32(INT_MIN))
        return jnp.where(cnt >= n_sel, cand_u, ans_u)

    ans_u = lax.fori_loop(0, 32, bisect, jnp.zeros((1, tq), I32))
    thr = ans_u ^ jnp.int32(INT_MIN)
    tied = jnp.max(jnp.abs(count_ge(thr) - n_sel)) > 0

    @pl.when(jnp.logical_not(tied))
    def _():
        def finish(c, carry):
            r0 = pl.multiple_of(c * tq, tq)
            key = ks_ref[pl.ds(r0, tq), :]
            mask_ref[pl.ds(r0, tq), :] = jnp.where(key >= thr, 0.0, NEG_INF).astype(BF16)
            return carry

        lax.fori_loop(0, i + 1, finish, 0)

    @pl.when(tied)
    def _():
        need = (n_sel - count_ge(thr + 1)).astype(F32)
        strict_lower = jnp.where(col < row, 1.0, 0.0).astype(BF16)

        def finish(c, seen):
            r0 = pl.multiple_of(c * tq, tq)
            key = ks_ref[pl.ds(r0, tq), :]
            eq = jnp.where(key == thr, 1.0, 0.0)
            before = jnp.dot(strict_lower, eq.astype(BF16), preferred_element_type=F32) + seen
            tie_ok = jnp.where(before < need, eq, 0.0)
            sel = jnp.where(key > thr, 1.0, tie_ok)
            mask_ref[pl.ds(r0, tq), :] = jnp.where(sel > 0.0, 0.0, NEG_INF).astype(BF16)
            return seen + jnp.sum(eq, axis=0, keepdims=True)

        lax.fori_loop(0, i + 1, finish, jnp.zeros((1, tq), F32))

    def fill(c, carry):
        r0 = pl.multiple_of(c * tq, tq)
        mask_ref[pl.ds(r0, tq), :] = jnp.full((tq, tq), NEG_INF, BF16)
        return carry

    lax.fori_loop(i + 1, n_blocks, fill, 0)


def _idx_prompt(ki, qt, w, *, n_sel, tq=128, cb=4):
    s, idim = ki.shape
    nb, n_heads, _ = w.shape
    assert nb * tq == s and (s // tq) % cb == 0 and cb * tq >= n_sel
    return pl.pallas_call(
        functools.partial(_idx_prompt_kernel, n_sel=n_sel, tq=tq, n_heads=n_heads, cb=cb),
        grid=(nb,),
        in_specs=[
            pl.BlockSpec((s, idim), lambda i: (0, 0)),
            pl.BlockSpec((1, idim, n_heads * tq), lambda i: (i, 0, 0)),
            pl.BlockSpec((1, n_heads, tq), lambda i: (i, 0, 0)),
        ],
        out_specs=pl.BlockSpec((s, tq), lambda i: (0, i)),
        out_shape=jax.ShapeDtypeStruct((s, s), BF16),
        scratch_shapes=[pltpu.VMEM((s, tq), I32)],
        compiler_params=_cparams(("arbitrary",)),
        name="idx_prompt",
    )(ki, qt, w)


def _attn_prompt_kernel(it_ref, jt_ref, qt_ref, k_ref, vt_ref, b1_ref, b2_ref, o_ref,
                        m_sc, l_sc, acc_sc, s_sc, mx_sc, *, mode, t, n_heads, hd, scale):
    s = pl.program_id(0)
    i = it_ref[s]
    j = jt_ref[s]

    @pl.when(j == 0)
    def _():
        m_sc[...] = jnp.full(m_sc.shape, M_INIT, F32)
        l_sc[...] = jnp.zeros_like(l_sc)
        acc_sc[...] = jnp.zeros_like(acc_sc)

    def logits(diagonal):
        if mode == "dsa":
            sel_bias = b1_ref[...].astype(F32)
            near = jnp.minimum(i - j, 2)
        for h in range(n_heads):
            hs = slice(h * hd, (h + 1) * hd)
            st = jnp.dot(k_ref[:, hs], qt_ref[hs, :], preferred_element_type=F32) * (scale * LOG2E)
            if mode == "fox":
                st = st - jnp.concatenate([b1_ref[h]] * (t // LANES), axis=1)
                if diagonal:
                    st = st + b2_ref[...]
            else:
                st = st + b2_ref[h, near] + sel_bias
            s_sc[h] = st
            mx_sc[h] = jnp.max(st, axis=0, keepdims=True)

    if mode == "fox":
        pl.when(j == i)(functools.partial(logits, True))
        pl.when(j != i)(functools.partial(logits, False))
    else:
        logits(None)

    for h in range(n_heads):
        hs = slice(h * hd, (h + 1) * hd)
        st = s_sc[h]
        m_old = m_sc[h]
        m_new = jnp.maximum(m_old, mx_sc[h])
        alpha = jnp.exp2(m_old - m_new)
        p = jnp.exp2(st - m_new)
        l_sc[h] = alpha * l_sc[h] + jnp.sum(p, axis=0, keepdims=True)
        pv = jnp.dot(vt_ref[hs, :], p.astype(BF16), preferred_element_type=F32)
        acc_sc[hs, :] = alpha * acc_sc[hs, :] + pv
        m_sc[h] = m_new

    @pl.when(j == i)
    def _():
        for h in range(n_heads):
            hs = slice(h * hd, (h + 1) * hd)
            o_ref[hs, :] = (acc_sc[hs, :] / l_sc[h]).astype(BF16)


def _attn_prompt(mode, qt, k, vt, b1, b2, *, t, n_heads, hd):
    width, s = qt.shape
    nb = s // t
    assert nb * t == s and t % LANES == 0
    pairs = [(i, j) for i in range(nb) for j in range(i + 1)]
    it = jnp.asarray([p[0] for p in pairs], I32)
    jt = jnp.asarray([p[1] for p in pairs], I32)
    if mode == "fox":
        b1_spec = pl.BlockSpec((n_heads, t, LANES), lambda g, it, jt: (0, jt[g], 0))
    else:
        b1_spec = pl.BlockSpec((t, t), lambda g, it, jt: (jt[g], it[g]))
    grid_spec = pltpu.PrefetchScalarGridSpec(
        num_scalar_prefetch=2,
        grid=(len(pairs),),
        in_specs=[
            pl.BlockSpec((width, t), lambda g, it, jt: (0, it[g])),
            pl.BlockSpec((t, width), lambda g, it, jt: (jt[g], 0)),
            pl.BlockSpec((width, t), lambda g, it, jt: (0, jt[g])),
            b1_spec,
            pl.BlockSpec(memory_space=pltpu.VMEM),
        ],
        out_specs=pl.BlockSpec((width, t), lambda g, it, jt: (0, it[g])),
        scratch_shapes=[
            pltpu.VMEM((n_heads, 1, t), F32),
            pltpu.VMEM((n_heads, 1, t), F32),
            pltpu.VMEM((width, t), F32),
            pltpu.VMEM((n_heads, t, t), F32),
            pltpu.VMEM((n_heads, 1, t), F32),
        ],
    )
    return pl.pallas_call(
        functools.partial(_attn_prompt_kernel, mode=mode, t=t, n_heads=n_heads, hd=hd,
                          scale=hd ** -0.5),
        grid_spec=grid_spec,
        out_shape=jax.ShapeDtypeStruct((width, s), BF16),
        compiler_params=_cparams(("arbitrary",)),
        name="attn_prompt_" + mode,
    )(it, jt, qt, k, vt, b1, b2)


def _idx_sample_kernel(pt_ref, *refs, n_sel, pps, page, n_steps, t_new, n_heads, group):
    page_refs = refs[:pps]
    qi_ref, w_ref, kn_ref, mask_ref, sc_sc = refs[pps:]
    st = pl.program_id(1)
    span = pps * page
    width = sc_sc.shape[1]
    n_past = n_steps * span
    qi = qi_ref[0]
    w = w_ref[0]

    def scores(kb):
        sc = lax.dot_general(qi, kb, (((1,), (1,)), ((), ())), preferred_element_type=F32)
        sc = jnp.maximum(sc, 0.0) * jnp.concatenate([w] * (kb.shape[0] // LANES), axis=1)
        acc = sc[0:SUBLANES, :]
        for h in range(1, n_heads):
            acc = acc + sc[h * SUBLANES:(h + 1) * SUBLANES, :]
        return acc

    slot = pl.program_id(0) % group
    r0 = pl.multiple_of(slot * SUBLANES, SUBLANES)
    kb = jnp.concatenate([r[0].astype(BF16) for r in page_refs], axis=0)
    c0 = pl.multiple_of(st * span, span)
    sc_sc[pl.ds(r0, SUBLANES), pl.ds(c0, span)] = _sort_key(scores(kb))

    @pl.when(st == n_steps - 1)
    def _():
        qrow = lax.broadcasted_iota(I32, (SUBLANES, page), 0)
        jcol = lax.broadcasted_iota(I32, (SUBLANES, page), 1)
        new = jnp.where((jcol <= qrow) & (jcol < t_new), scores(kn_ref[0]), NEG_INF)
        sc_sc[pl.ds(r0, SUBLANES), n_past:n_past + page] = _sort_key(new)

    @pl.when((st == n_steps - 1) & (slot == group - 1))
    def _():
        rows = group * SUBLANES

        def bisect(b, ans_u):
            cand_u = ans_u | jnp.left_shift(jnp.int32(1), 31 - b)
            cand = cand_u ^ jnp.int32(INT_MIN)
            cnt = jnp.sum(jnp.where(sc_sc[...] >= cand, 1, 0).astype(I32), axis=1, keepdims=True)
            return jnp.where(cnt >= n_sel, cand_u, ans_u)

        ans_u = lax.fori_loop(0, 32, bisect, jnp.zeros((rows, 1), I32))
        thr = ans_u ^ jnp.int32(INT_MIN)
        n_gt = jnp.sum(jnp.where(sc_sc[...] > thr, 1, 0).astype(I32), axis=1, keepdims=True)
        need = (n_sel - n_gt).astype(F32)

        r = lax.broadcasted_iota(I32, (LANES, LANES), 0)
        c = lax.broadcasted_iota(I32, (LANES, LANES), 1)
        strict_upper = jnp.where(r < c, 1.0, 0.0).astype(BF16)
        seen = jnp.zeros((rows, 1), F32)
        for blk in range(width // LANES):
            cols = slice(blk * LANES, (blk + 1) * LANES)
            kblk = sc_sc[:, cols]
            eq = jnp.where(kblk == thr, 1.0, 0.0)
            before = jnp.dot(eq.astype(BF16), strict_upper, preferred_element_type=F32) + seen
            tie_ok = jnp.where(before < need, eq, 0.0)
            sel = jnp.where(kblk > thr, 1.0, tie_ok)
            mask_ref[:, :, cols] = jnp.where(sel > 0.0, 0.0, NEG_INF).reshape(group, SUBLANES, LANES)
            seen = seen + jnp.sum(eq, axis=1, keepdims=True)


def _idx_sample(pt_flat, pool_ik, qi, w, kn, *, n_sel, n_pages, t_new, n_heads, pps=8, group=8):
    b = qi.shape[0]
    page, idim = pool_ik.shape[1:]
    pps = math.gcd(pps, n_pages)
    group = math.gcd(group, b)
    assert page == LANES
    n_steps = n_pages // pps
    width = n_pages * page + page

    def page_spec(r):
        return pl.BlockSpec((1, page, idim),
                            lambda bi, st, pt: (pt[bi * n_pages + st * pps + r], 0, 0))

    grid_spec = pltpu.PrefetchScalarGridSpec(
        num_scalar_prefetch=1,
        grid=(b, n_steps),
        in_specs=[page_spec(r) for r in range(pps)] + [
            pl.BlockSpec((1, n_heads * SUBLANES, idim), lambda bi, st, pt: (bi, 0, 0)),
            pl.BlockSpec((1, n_heads * SUBLANES, LANES), lambda bi, st, pt: (bi, 0, 0)),
            pl.BlockSpec((1, page, idim), lambda bi, st, pt: (bi, 0, 0)),
        ],
        out_specs=pl.BlockSpec((group, SUBLANES, width), lambda bi, st, pt: (bi // group, 0, 0)),
        scratch_shapes=[pltpu.VMEM((group * SUBLANES, width), I32)],
    )
    return pl.pallas_call(
        functools.partial(_idx_sample_kernel, n_sel=n_sel, pps=pps, page=page, n_steps=n_steps,
                          t_new=t_new, n_heads=n_heads, group=group),
        grid_spec=grid_spec,
        out_shape=jax.ShapeDtypeStruct((b, SUBLANES, width), F32),
        compiler_params=_cparams(("arbitrary", "arbitrary")),
        name="idx_sample",
    )(pt_flat, *([pool_ik] * pps), qi, w, kn)


def _attn_sample_kernel(pt_ref, *refs, mode, pps, page, n_steps, t_new, n_heads, hd, scale):
    k_refs = refs[:pps]
    v_refs = refs[pps:2 * pps]
    rest = refs[2 * pps:]
    if mode == "fox":
        lf_refs = rest[:pps]
        q_ref, kn_ref, vn_ref, lfn_ref, o_ref, m_sc, l_sc, acc_sc, carry_sc = rest[pps:]
    else:
        q_ref, kn_ref, vn_ref, mask_ref, t5_ref, o_ref, m_sc, l_sc, acc_sc = rest
    st = pl.program_id(1)
    rows = SUBLANES * n_heads
    n_past = n_steps * pps * page

    @pl.when(st == 0)
    def _():
        m_sc[...] = jnp.full(m_sc.shape, M_INIT, F32)
        l_sc[...] = jnp.zeros_like(l_sc)
        acc_sc[...] = jnp.zeros_like(acc_sc)
        if mode == "fox":
            carry_sc[...] = jnp.zeros_like(carry_sc)

    span = pps * page
    is_last = st == n_steps - 1

    if mode == "fox":
        ru = lax.broadcasted_iota(I32, (page, page), 0)
        cu = lax.broadcasted_iota(I32, (page, page), 1)
        lower_incl = jnp.where(cu <= ru, 1.0, 0.0).astype(BF16)
        blocks = []
        before = jnp.zeros((1, 3 * n_heads), F32)
        for r in list(lf_refs) + [lfn_ref]:
            c = jnp.dot(lower_incl, jnp.concatenate(_split3(r[0]), axis=1), preferred_element_type=F32)
            blocks.append(c + before)
            before = before + c[page - 1:page, :]
        cum_parts = jnp.concatenate(_split3(jnp.concatenate(blocks, axis=0)), axis=1)
        rh = lax.broadcasted_iota(I32, (rows, 9 * n_heads), 0) // SUBLANES
        ch = lax.broadcasted_iota(I32, (rows, 9 * n_heads), 1) % n_heads
        head_of_row = jnp.where(rh == ch, 1.0, 0.0).astype(BF16)
        cum_t = lax.dot_general(head_of_row, cum_parts, (((1,), (1,)), ((), ())),
                                preferred_element_type=F32)
        f_cum = cum_t + carry_sc[...]
        carry_sc[...] = f_cum[:, span - 1:span]
        qrow = lax.broadcasted_iota(I32, (rows, page), 0) % SUBLANES
        jcol = lax.broadcasted_iota(I32, (rows, page), 1)
        new_ok = (jcol <= qrow) & (jcol < t_new) & is_last
        bias = jnp.concatenate([-f_cum[:, :span], jnp.where(new_ok, -f_cum[:, span:], NEG_INF)], axis=1)
    else:
        def mask_rows(mk):
            return jnp.concatenate([mk] * n_heads, axis=0)

        c0 = pl.multiple_of(st * span, span)
        past_bias = t5_ref[:, pl.ds(c0, span)] + mask_rows(mask_ref[0, :, pl.ds(c0, span)])
        new_bias = t5_ref[:, n_past:n_past + page] + mask_rows(mask_ref[0, :, n_past:n_past + page])
        bias = jnp.concatenate([past_bias, jnp.where(is_last, new_bias, NEG_INF)], axis=1)

    def head_rows(page_refs, new_ref, h):
        return jnp.concatenate(
            [r[0, pl.ds(h, page, stride=n_heads), :].astype(BF16) for r in page_refs] + [new_ref[0, h]], axis=0)

    s = jnp.concatenate(
        [lax.dot_general(q_ref[0, h * SUBLANES:(h + 1) * SUBLANES, :], head_rows(k_refs, kn_ref, h),
                         (((1,), (1,)), ((), ())), preferred_element_type=F32) for h in range(n_heads)], axis=0)
    s = s * scale + bias
    m_old = m_sc[...]
    m_new = jnp.maximum(m_old, jnp.max(s, axis=1, keepdims=True))
    alpha = jnp.exp(m_old - m_new)
    p = jnp.exp(s - m_new)
    l_sc[...] = alpha * l_sc[...] + jnp.sum(p, axis=1, keepdims=True)
    p16 = p.astype(BF16)
    pv = jnp.concatenate(
        [jnp.dot(p16[h * SUBLANES:(h + 1) * SUBLANES, :], head_rows(v_refs, vn_ref, h),
                 preferred_element_type=F32) for h in range(n_heads)], axis=0)
    acc_sc[...] = alpha * acc_sc[...] + pv
    m_sc[...] = m_new

    @pl.when(is_last)
    def _():
        o_ref[0] = (acc_sc[...] / l_sc[...]).astype(BF16)


def _attn_sample(mode, pt_flat, pool_k, pool_v, q, kn, vn, extra, *, n_pages, t_new, n_heads, hd, pps=8):
    b, rows, _ = q.shape
    page = pool_k.shape[1] // n_heads
    pps = math.gcd(pps, n_pages)
    assert page == LANES and rows == SUBLANES * n_heads and t_new <= SUBLANES
    n_steps = n_pages // pps

    def page_spec(shape_tail, r):
        zeros = (0,) * len(shape_tail)
        return pl.BlockSpec((1,) + shape_tail,
                            lambda bi, st, pt: (pt[bi * n_pages + st * pps + r],) + zeros)

    def per_req(shape_tail):
        zeros = (0,) * len(shape_tail)
        return pl.BlockSpec((1,) + shape_tail, lambda bi, st, pt: (bi,) + zeros)

    in_specs = [page_spec((page * n_heads, hd), r) for r in range(pps)] * 2
    args = [pool_k] * pps + [pool_v] * pps
    scratch = [pltpu.VMEM((rows, 1), F32), pltpu.VMEM((rows, 1), F32), pltpu.VMEM((rows, hd), F32)]
    shared = [per_req((rows, hd)), per_req((n_heads, page, hd)), per_req((n_heads, page, hd))]
    if mode == "fox":
        pool_lf, lfn = extra
        in_specs += [page_spec((page, n_heads), r) for r in range(pps)]
        args += [pool_lf] * pps
        in_specs += shared + [per_req((page, n_heads))]
        args += [q, kn, vn, lfn]
        scratch.append(pltpu.VMEM((rows, 1), F32))
    else:
        mask, t5 = extra
        in_specs += shared + [per_req(mask.shape[1:]), pl.BlockSpec(memory_space=pltpu.VMEM)]
        args += [q, kn, vn, mask, t5]
    grid_spec = pltpu.PrefetchScalarGridSpec(
        num_scalar_prefetch=1,
        grid=(b, n_steps),
        in_specs=in_specs,
        out_specs=pl.BlockSpec((1, rows, hd), lambda bi, st, pt: (bi, 0, 0)),
        scratch_shapes=scratch,
    )
    return pl.pallas_call(
        functools.partial(_attn_sample_kernel, mode=mode, pps=pps, page=page, n_steps=n_steps,
                          t_new=t_new, n_heads=n_heads, hd=hd, scale=hd ** -0.5),
        grid_spec=grid_spec,
        out_shape=jax.ShapeDtypeStruct((b, rows, hd), BF16),
        compiler_params=_cparams(("arbitrary", "arbitrary")),
        name="attn_sample_" + mode,
    )(pt_flat, *args)


def _t5_bucket(dist, num_buckets):
    n = jnp.maximum(dist, 0)
    exact = num_buckets // 2
    nf = jnp.maximum(n, 1).astype(F32)
    large = exact + (jnp.log(nf / exact) / math.log(MAX_DISTANCE / exact)
                     * (num_buckets - exact)).astype(I32)
    return jnp.where(n < exact, n, jnp.minimum(large, num_buckets - 1))


def _t5_bias(rel_bias, dist):
    nb = rel_bias.shape[0]
    onehot = (_t5_bucket(dist, nb)[..., None] == jnp.arange(nb)).astype(F32)
    b = jnp.einsum("...n,nh->h...", onehot, rel_bias.astype(F32), precision=lax.Precision.HIGHEST)
    return jnp.where(dist[None] >= 0, b, NEG_INF)


def _is_pow2(x):
    return math.frexp(x)[0] == 0.5


def _mix_prompt(q_a, k_a, v_a, lf_a, q_b, k_b, v_b, q_i, k_i, w_i, rel_bias, *, n_heads, hd, t=256):
    s = q_a.shape[0]
    idim = k_i.shape[1]
    ih = w_i.shape[1]
    n_sel = min(TOPK_MAX, s // 4)
    assert MAX_DISTANCE <= t + 1
    idx_scale, idx_w_scale = idim ** -0.5, ih ** -0.5
    assert _is_pow2(idx_scale) and _is_pow2(idx_w_scale)

    fk = _cumsum_rep(lf_a, out_scale=LOG2E)
    r = jnp.arange(t)
    dist0 = r[None, :] - r[:, None]
    causal = jnp.where(dist0 >= 0, 0.0, NEG_INF).astype(F32)
    o_a = _attn_prompt("fox", q_a.T, k_a, v_a.T, fk, causal, t=t, n_heads=n_heads, hd=hd).T

    tq = LANES
    nb = s // tq
    qt = q_i.reshape(nb, tq, ih, idim).transpose(0, 3, 2, 1).reshape(nb, idim, ih * tq)
    w = (w_i * (idx_scale * idx_w_scale)).reshape(nb, tq, ih).transpose(0, 2, 1)
    mask = _idx_prompt(k_i, qt, w, n_sel=n_sel, tq=tq)
    btiles = jnp.stack([_t5_bias(rel_bias, dist0 + nd * t) for nd in range(3)], axis=1) * LOG2E
    o_b = _attn_prompt("dsa", q_b.T, k_b, v_b.T, mask, btiles, t=t, n_heads=n_heads, hd=hd).T
    return o_a, o_b


def _head_major(x, n_heads, hd, t_pad):
    b, t_new, _ = x.shape
    xh = x.reshape(b, t_new, n_heads, hd).transpose(0, 2, 1, 3)
    return jnp.pad(xh, ((0, 0), (0, 0), (0, t_pad - t_new), (0, 0)))


def _pad_rows(x, rows):
    return jnp.pad(x, ((0, 0), (0, rows - x.shape[1]), (0, 0)))


def _mix_sample(q_a, k_a, v_a, lf_a, q_b, k_b, v_b, q_i, k_i, w_i, rel_bias, page_table,
                fox_k, fox_v, fox_lf, dsa_k, dsa_v, idx_k, *, layer, n_heads, hd):
    b, t_new, width = q_a.shape
    n_pages = page_table.shape[1]
    depth, n_pool, page = fox_k.shape[:3]
    idim = k_i.shape[2]
    ih = w_i.shape[2]
    past = n_pages * page
    n_sel = min(TOPK_MAX, (past + t_new) // 4)
    idx_scale, idx_w_scale = idim ** -0.5, ih ** -0.5
    assert _is_pow2(idx_scale) and _is_pow2(idx_w_scale)
    assert t_new <= SUBLANES
    pt_flat = page_table.reshape(-1).astype(I32) + layer * n_pool
    n_pool = depth * n_pool
    fox_lf = fox_lf.reshape(n_pool, page, n_heads)
    idx_k = idx_k.reshape(n_pool, page, idim)

    flat = lambda pool: pool.reshape(n_pool, page * n_heads, hd)
    q_rows = lambda q: _head_major(q, n_heads, hd, SUBLANES).reshape(b, n_heads * SUBLANES, hd)
    new_keys = lambda x: _head_major(x, n_heads, hd, page)
    o_a = _attn_sample("fox", pt_flat, flat(fox_k), flat(fox_v), q_rows(q_a), new_keys(k_a), new_keys(v_a),
                       (fox_lf, _pad_rows(lf_a, page)), n_pages=n_pages, t_new=t_new, n_heads=n_heads, hd=hd)

    qi = jnp.pad(q_i.reshape(b, t_new, ih, idim), ((0, 0), (0, SUBLANES - t_new), (0, 0), (0, 0)))
    qi = qi.transpose(0, 2, 1, 3).reshape(b, ih * SUBLANES, idim)
    wr = jnp.pad(w_i * (idx_scale * idx_w_scale), ((0, 0), (0, SUBLANES - t_new), (0, 0)))
    wr = jnp.broadcast_to(wr.transpose(0, 2, 1).reshape(b, ih * SUBLANES, 1), (b, ih * SUBLANES, LANES))
    mask = _idx_sample(pt_flat, idx_k, qi, wr, _pad_rows(k_i, page), n_sel=n_sel, n_pages=n_pages,
                       t_new=t_new, n_heads=ih)
    kpos = jnp.arange(past + page)
    qpos = past + jnp.arange(t_new)
    dist = jnp.where(kpos[None, :] < past + t_new, qpos[:, None] - kpos[None, :], -1)
    t5 = _t5_bias(rel_bias, dist)
    t5 = jnp.pad(t5, ((0, 0), (0, SUBLANES - t_new), (0, 0))).reshape(n_heads * SUBLANES, past + page)
    o_b = _attn_sample("dsa", pt_flat, flat(dsa_k), flat(dsa_v), q_rows(q_b), new_keys(k_b), new_keys(v_b),
                       (mask, t5), n_pages=n_pages, t_new=t_new, n_heads=n_heads, hd=hd)
    tokens = lambda o: o.reshape(b, n_heads, SUBLANES, hd)[:, :, :t_new].transpose(0, 2, 1, 3).reshape(b * t_new, width)
    return tokens(o_a), tokens(o_b)


def kernel(x_prompt, x_sample, cache_fox_k, cache_fox_v, cache_fox_logf, cache_dsa_k, cache_dsa_v, cache_idx_k, page_table, p_prompt, p_sample, rel_bias, g_ffn1, w1_pre, w3_pre, w2_pre, g_mix, w_in, b_forget, w_branch_fox, w_branch_dsa, w_out, g_ffn2, w1_post, w3_post, w2_post, g_ple, w_ple, w_ple_gate, g_final):
    depth = w_in.shape[0]
    bp, sp, d = x_prompt.shape
    bs, ts, _ = x_sample.shape
    assert bp == 1
    n_fox, hd = cache_fox_k.shape[3:]
    n_dsa = cache_dsa_k.shape[3]
    assert n_fox == n_dsa
    idim = cache_idx_k.shape[-1]
    ih = IDX_HEADS
    fw, dw = n_fox * hd, n_dsa * hd
    n_in = w_in.shape[2]
    assert n_in == 3 * fw + n_fox + 3 * dw + ih * idim + idim + ih + 2 * d
    mp, ms = bp * sp, bs * ts
    small_w = n_fox + idim + ih
    assert small_w <= LANES

    x = jnp.concatenate([x_prompt.reshape(mp, d), x_sample.reshape(ms, d)], axis=0)
    st_p, st_s = [], []
    for l in range(depth):
        offs = np.cumsum([0, fw, fw, fw, n_fox, dw, dw, dw, ih * idim, idim, ih, 2 * d])
        col = lambda a, c: w_in[l][:, offs[a]:offs[a] + c]
        w_qkv = jnp.concatenate([col(0, fw), col(4, dw), col(7, ih * idim),
                                 col(1, fw), col(2, fw), col(5, dw), col(6, dw)], axis=1).astype(BF16)
        w_gate = col(10, 2 * d).astype(BF16)
        w_small = jnp.concatenate([col(3, n_fox), col(8, idim), col(9, ih)], axis=1)
        w_small = jnp.pad(w_small, ((0, 0), (0, LANES - small_w))).astype(BF16)
        b_small = jnp.pad(b_forget[l], (0, LANES - n_fox)).reshape(1, LANES)

        x1, h_mix = _ffn(x, g_ffn1[l], w1_pre[l].astype(BF16), w3_pre[l].astype(BF16),
                         w2_pre[l].astype(BF16), g_mix[l])
        z32, z16 = _proj("qkv", h_mix, w_qkv)
        gate = _proj("gate", h_mix, w_gate)
        small = _proj("small", h_mix, w_small, bias=b_small, n_fox=n_fox)

        q_a, q_b, q_i = z16[:, :fw], z16[:, fw:fw + dw], z16[:, fw + dw:fw + dw + ih * idim]
        kv0 = fw + dw + ih * idim
        kv16 = [z16[:, kv0 + n * fw:kv0 + (n + 1) * fw] for n in range(4)]
        kv32 = [z32[:, kv0 + n * fw:kv0 + (n + 1) * fw] for n in range(4)]
        lf = small[:, :n_fox]
        k_i32 = small[:, n_fox:n_fox + idim]
        w_i = small[:, n_fox + idim:small_w]
        k_i16 = k_i32.astype(BF16)

        o_ap, o_bp = _mix_prompt(q_a[:mp], kv16[0][:mp], kv16[1][:mp], lf[:mp], q_b[:mp], kv16[2][:mp],
                                 kv16[3][:mp], q_i[:mp], k_i16[:mp], w_i[:mp], rel_bias,
                                 n_heads=n_fox, hd=hd)
        rs = lambda a: a[mp:].reshape(bs, ts, -1)
        o_as, o_bs = _mix_sample(rs(q_a), rs(kv16[0]), rs(kv16[1]), rs(lf), rs(q_b), rs(kv16[2]),
                                 rs(kv16[3]), rs(q_i), rs(k_i16), rs(w_i), rel_bias, page_table,
                                 cache_fox_k, cache_fox_v, cache_fox_logf, cache_dsa_k,
                                 cache_dsa_v, cache_idx_k, layer=l, n_heads=n_fox, hd=hd)
        o_a = jnp.concatenate([o_ap, o_as], axis=0)
        o_b = jnp.concatenate([o_bp, o_bs], axis=0)

        merged = _merge(o_a, o_b, gate, w_branch_fox[l].astype(BF16), w_branch_dsa[l].astype(BF16))
        x2 = _resid_mm(x1, merged, w_out[l].astype(BF16))
        x3, h_ple = _ffn(x2, g_ffn2[l], w1_post[l].astype(BF16), w3_post[l].astype(BF16),
                         w2_post[l].astype(BF16), g_ple[l])
        p = jnp.concatenate([p_prompt[l].reshape(mp, -1), p_sample[l].reshape(ms, -1)], axis=0).astype(BF16)
        x, y = _ple(x3, p, h_ple, w_ple[l].astype(BF16), w_ple_gate[l].astype(BF16), g_final)

        heads = lambda a, bb, tt: a.reshape(bb, tt, n_fox, hd)
        st_p.append((heads(kv32[0][:mp], bp, sp), heads(kv32[1][:mp], bp, sp), lf[:mp].reshape(bp, sp, n_fox),
                     heads(kv32[2][:mp], bp, sp), heads(kv32[3][:mp], bp, sp), k_i32[:mp].reshape(bp, sp, idim)))
        st_s.append((heads(kv32[0][mp:], bs, ts), heads(kv32[1][mp:], bs, ts), lf[mp:].reshape(bs, ts, n_fox),
                     heads(kv32[2][mp:], bs, ts), heads(kv32[3][mp:], bs, ts), k_i32[mp:].reshape(bs, ts, idim)))

    pfk, pfv, pflf, pdk, pdv, pik = [jnp.stack(a) for a in zip(*st_p)]
    sfk, sfv, sflf, sdk, sdv, sik = [jnp.stack(a) for a in zip(*st_s)]
    y_prompt = y[:mp].reshape(bp, sp, d)
    y_sample = y[mp:].reshape(bs, ts, d)
    return (y_prompt, y_sample, pfk, pfv, pflf, pdk, pdv, pik, sfk, sfv, sflf, sdk, sdv, sik)
```

```python
import functools
import math

import numpy as np
import jax
import jax.numpy as jnp
from jax import lax
from jax.experimental import pallas as pl
from jax.experimental.pallas import tpu as pltpu

F32 = jnp.float32
BF16 = jnp.bfloat16
I32 = jnp.int32

TOPK_MAX = 256
MAX_DISTANCE = 128
RMS_EPS = 1e-6
IDX_HEADS = 16

V7X_VMEM_BYTES = 64 * 1024 * 1024
VMEM_LIMIT = V7X_VMEM_BYTES - 8 * 1024 * 1024
LANES = 128
SUBLANES = 8

LOG2E = math.log2(math.e)
NEG_INF = float("-inf")
M_INIT = -1e30
INT_MIN = -2 ** 31
NEG_INF_KEY = int(np.int32(np.uint32(0xFF800000) ^ np.uint32(0x7FFFFFFF)))


def _cparams(sem):
    return pltpu.CompilerParams(dimension_semantics=sem, vmem_limit_bytes=VMEM_LIMIT)


def _rms(x, g):
    ms = jnp.mean(x * x, axis=-1, keepdims=True)
    return x * lax.rsqrt(ms + RMS_EPS) * g


def _sigmoid(x):
    return 1.0 / (1.0 + jnp.exp(-x))


def _split3(x):
    hi = x.astype(BF16)
    r1 = x - hi.astype(F32)
    mid = r1.astype(BF16)
    lo = (r1 - mid.astype(F32)).astype(BF16)
    return hi, mid, lo


def _sort_key(x):
    bits = pltpu.bitcast(x, I32)
    return bits ^ ((bits >> 31) & jnp.int32(0x7FFFFFFF))


def _ffn_kernel(x_ref, g_ref, w1_ref, w3_ref, w2_ref, gn_ref, o_ref, hn_ref, h_sc, acc_sc):
    f = pl.program_id(1)

    @pl.when(f == 0)
    def _():
        h_sc[...] = _rms(x_ref[...], g_ref[...]).astype(BF16)
        acc_sc[...] = jnp.zeros_like(acc_sc)

    h = h_sc[...]
    a = jnp.dot(h, w1_ref[...], preferred_element_type=F32)
    b = jnp.dot(h, w3_ref[...], preferred_element_type=F32)
    act = (a * _sigmoid(a) * b).astype(BF16)
    acc_sc[...] += jnp.dot(act, w2_ref[...], preferred_element_type=F32)

    @pl.when(f == pl.num_programs(1) - 1)
    def _():
        x1 = x_ref[...] + 0.5 * acc_sc[...]
        o_ref[...] = x1
        hn_ref[...] = _rms(x1, gn_ref[...]).astype(BF16)


def _ffn(x, g, w1, w3, w2, g_next, *, tm=512, tf=512):
    m, d = x.shape
    dff = w1.shape[1]
    assert m % tm == 0 and dff % tf == 0
    return pl.pallas_call(
        _ffn_kernel,
        grid=(m // tm, dff // tf),
        in_specs=[
            pl.BlockSpec((tm, d), lambda i, f: (i, 0)),
            pl.BlockSpec((1, d), lambda i, f: (0, 0)),
            pl.BlockSpec((d, tf), lambda i, f: (0, f)),
            pl.BlockSpec((d, tf), lambda i, f: (0, f)),
            pl.BlockSpec((tf, d), lambda i, f: (f, 0)),
            pl.BlockSpec((1, d), lambda i, f: (0, 0)),
        ],
        out_specs=[
            pl.BlockSpec((tm, d), lambda i, f: (i, 0)),
            pl.BlockSpec((tm, d), lambda i, f: (i, 0)),
        ],
        out_shape=[jax.ShapeDtypeStruct((m, d), F32), jax.ShapeDtypeStruct((m, d), BF16)],
        scratch_shapes=[pltpu.VMEM((tm, d), BF16), pltpu.VMEM((tm, d), F32)],
        compiler_params=_cparams(("parallel", "arbitrary")),
        name="ffn",
    )(x, g.reshape(1, d), w1, w3, w2, g_next.reshape(1, d))


def _proj_q_kernel(h_ref, w_ref, o16_ref):
    o16_ref[...] = jnp.dot(h_ref[...], w_ref[...], preferred_element_type=F32).astype(BF16)


def _proj_kv_kernel(h_ref, w_ref, op_ref, os_ref, o16_ref, *, n_heads, hd, prompt_tiles):
    i = pl.program_id(0)
    z = jnp.dot(h_ref[...], w_ref[...], preferred_element_type=F32)
    o16_ref[...] = z.astype(BF16)

    def put(dst):
        for h in range(n_heads):
            dst[:, h, :] = z[:, h * hd:(h + 1) * hd]

    pl.when(i < prompt_tiles)(functools.partial(put, op_ref))
    pl.when(i >= prompt_tiles)(functools.partial(put, os_ref))


def _proj_kv(h, w, *, m_prompt, n_heads, hd, tm=512):
    m, k = h.shape
    n = w.shape[1]
    m_sample = m - m_prompt
    assert n == n_heads * hd and m_prompt % tm == 0 and m_sample % tm == 0
    pt = m_prompt // tm
    return pl.pallas_call(
        functools.partial(_proj_kv_kernel, n_heads=n_heads, hd=hd, prompt_tiles=pt),
        grid=(m // tm,),
        in_specs=[pl.BlockSpec((tm, k), lambda i: (i, 0)), pl.BlockSpec((k, n), lambda i: (0, 0))],
        out_specs=[
            pl.BlockSpec((tm, n_heads, hd), lambda i: (jnp.minimum(i, pt - 1), 0, 0)),
            pl.BlockSpec((tm, n_heads, hd), lambda i: (jnp.maximum(i - pt, 0), 0, 0)),
            pl.BlockSpec((tm, n), lambda i: (i, 0)),
        ],
        out_shape=[jax.ShapeDtypeStruct((m_prompt, n_heads, hd), F32),
                   jax.ShapeDtypeStruct((m_sample, n_heads, hd), F32),
                   jax.ShapeDtypeStruct((m, n), BF16)],
        compiler_params=_cparams(("arbitrary",)),
        name="proj_kv",
    )(h, w)


def _proj_gate_kernel(h_ref, w_ref, o_ref):
    z = jnp.dot(h_ref[...], w_ref[...], preferred_element_type=F32)
    o_ref[...] = _sigmoid(z)


def _proj_small_kernel(h_ref, w_ref, b_ref, o_ref, *, n_fox):
    z = jnp.dot(h_ref[...], w_ref[...], preferred_element_type=F32)
    lane = lax.broadcasted_iota(I32, z.shape, 1)
    zf = z + b_ref[...]
    logsig = jnp.minimum(zf, 0.0) - jnp.log1p(jnp.exp(-jnp.abs(zf)))
    o_ref[...] = jnp.where(lane < n_fox, logsig, z)


def _proj(kind, h, w, *, tm=512, tn=1024, bias=None, n_fox=0):
    m, k = h.shape
    n = w.shape[1]
    tn = min(tn, n)
    assert m % tm == 0 and n % tn == 0
    grid = (n // tn, m // tm)
    h_spec = pl.BlockSpec((tm, k), lambda j, i: (i, 0))
    w_spec = pl.BlockSpec((k, tn), lambda j, i: (0, j))
    o_spec = pl.BlockSpec((tm, tn), lambda j, i: (i, j))
    cp = _cparams(("parallel", "parallel"))
    if kind == "q":
        return pl.pallas_call(
            _proj_q_kernel, grid=grid, in_specs=[h_spec, w_spec], out_specs=o_spec,
            out_shape=jax.ShapeDtypeStruct((m, n), BF16), compiler_params=cp, name="proj_q")(h, w)
    if kind == "gate":
        return pl.pallas_call(
            _proj_gate_kernel, grid=grid, in_specs=[h_spec, w_spec], out_specs=o_spec,
            out_shape=jax.ShapeDtypeStruct((m, n), F32), compiler_params=cp, name="proj_gate")(h, w)
    assert kind == "small"
    b_spec = pl.BlockSpec((1, tn), lambda j, i: (0, j))
    return pl.pallas_call(
        functools.partial(_proj_small_kernel, n_fox=n_fox), grid=grid,
        in_specs=[h_spec, w_spec, b_spec], out_specs=o_spec,
        out_shape=jax.ShapeDtypeStruct((m, n), F32), compiler_params=cp, name="proj_small")(h, w, bias)


def _merge_kernel(oa_ref, ob_ref, g_ref, wa_ref, wb_ref, m_ref, *, d):
    ua = jnp.dot(oa_ref[...], wa_ref[...], preferred_element_type=F32)
    ub = jnp.dot(ob_ref[...], wb_ref[...], preferred_element_type=F32)
    m_ref[...] = (g_ref[:, :d] * ua + g_ref[:, d:] * ub).astype(BF16)


def _merge(oa, ob, gate, wa, wb, *, tm=256):
    m, wdt = oa.shape
    d = wa.shape[1]
    assert m % tm == 0
    return pl.pallas_call(
        functools.partial(_merge_kernel, d=d),
        grid=(m // tm,),
        in_specs=[
            pl.BlockSpec((tm, wdt), lambda i: (i, 0)),
            pl.BlockSpec((tm, wdt), lambda i: (i, 0)),
            pl.BlockSpec((tm, 2 * d), lambda i: (i, 0)),
            pl.BlockSpec((wdt, d), lambda i: (0, 0)),
            pl.BlockSpec((wdt, d), lambda i: (0, 0)),
        ],
        out_specs=pl.BlockSpec((tm, d), lambda i: (i, 0)),
        out_shape=jax.ShapeDtypeStruct((m, d), BF16),
        compiler_params=_cparams(("parallel",)),
        name="merge",
    )(oa, ob, gate, wa, wb)


def _resid_mm_kernel(x_ref, m_ref, w_ref, o_ref):
    o_ref[...] = x_ref[...] + jnp.dot(m_ref[...], w_ref[...], preferred_element_type=F32)


def _resid_mm(x, mm, w, *, tm=256):
    m, d = x.shape
    k = mm.shape[1]
    assert m % tm == 0
    return pl.pallas_call(
        _resid_mm_kernel,
        grid=(m // tm,),
        in_specs=[
            pl.BlockSpec((tm, d), lambda i: (i, 0)),
            pl.BlockSpec((tm, k), lambda i: (i, 0)),
            pl.BlockSpec((k, d), lambda i: (0, 0)),
        ],
        out_specs=pl.BlockSpec((tm, d), lambda i: (i, 0)),
        out_shape=jax.ShapeDtypeStruct((m, d), F32),
        compiler_params=_cparams(("parallel",)),
        name="resid_mm",
    )(x, mm, w)


def _ple_kernel(x_ref, p_ref, h_ref, wp_ref, wg_ref, gn_ref, *o_refs, prompt_tiles):
    e = jnp.dot(p_ref[...], wp_ref[...], preferred_element_type=F32)
    gt = jnp.dot(h_ref[...], wg_ref[...], preferred_element_type=F32)
    x4 = x_ref[...] + e * _sigmoid(gt)
    if prompt_tiles is None:
        o_refs[0][...] = x4
    else:
        i = pl.program_id(0)
        y = _rms(x4, gn_ref[...])

        @pl.when(i < prompt_tiles)
        def _():
            o_refs[0][...] = y

        @pl.when(i >= prompt_tiles)
        def _():
            o_refs[1][...] = y


def _ple(x, p, h, wp, wg, g_final, *, m_prompt=None, tm=256):
    m, d = x.shape
    pd = p.shape[1]
    assert m % tm == 0
    if m_prompt is None:
        pt = None
        out_specs = [pl.BlockSpec((tm, d), lambda i: (i, 0))]
        out_shape = [jax.ShapeDtypeStruct((m, d), F32)]
    else:
        assert m_prompt % tm == 0
        pt = m_prompt // tm
        out_specs = [pl.BlockSpec((tm, d), lambda i: (jnp.minimum(i, pt - 1), 0)),
                     pl.BlockSpec((tm, d), lambda i: (jnp.maximum(i - pt, 0), 0))]
        out_shape = [jax.ShapeDtypeStruct((m_prompt, d), F32), jax.ShapeDtypeStruct((m - m_prompt, d), F32)]
    return pl.pallas_call(
        functools.partial(_ple_kernel, prompt_tiles=pt),
        grid=(m // tm,),
        in_specs=[
            pl.BlockSpec((tm, d), lambda i: (i, 0)),
            pl.BlockSpec((tm, pd), lambda i: (i, 0)),
            pl.BlockSpec((tm, d), lambda i: (i, 0)),
            pl.BlockSpec((pd, d), lambda i: (0, 0)),
            pl.BlockSpec((d, d), lambda i: (0, 0)),
            pl.BlockSpec((1, d), lambda i: (0, 0)),
        ],
        out_specs=out_specs,
        out_shape=out_shape,
        compiler_params=_cparams(("arbitrary",)),
        name="ple",
    )(x, p, h, wp, wg, g_final.reshape(1, d))


def _cumsum_kernel(lf_ref, o_ref, carry_sc, *, n_heads, tb, out_scale):
    @pl.when(pl.program_id(0) == 0)
    def _():
        carry_sc[...] = jnp.zeros_like(carry_sc)

    row = lax.broadcasted_iota(I32, (tb, tb), 0)
    col = lax.broadcasted_iota(I32, (tb, tb), 1)
    tri = jnp.where(col <= row, 1.0, 0.0).astype(F32)
    lf = lf_ref[...]
    for h in range(n_heads):
        lfb = jnp.broadcast_to(lf[:, h:h + 1], (tb, LANES))
        cum = jnp.dot(tri, lfb, preferred_element_type=F32, precision=lax.Precision.HIGHEST)
        cum = cum + carry_sc[h:h + 1, :]
        o_ref[h] = cum * out_scale
        carry_sc[h:h + 1, :] = cum[tb - 1:tb, :]


def _cumsum_rep(lf, *, out_scale, tb=256):
    s, n_heads = lf.shape
    assert s % tb == 0
    return pl.pallas_call(
        functools.partial(_cumsum_kernel, n_heads=n_heads, tb=tb, out_scale=out_scale),
        grid=(s // tb,),
        in_specs=[pl.BlockSpec((tb, n_heads), lambda i: (i, 0))],
        out_specs=pl.BlockSpec((n_heads, tb, LANES), lambda i: (0, i, 0)),
        out_shape=jax.ShapeDtypeStruct((n_heads, s, LANES), F32),
        scratch_shapes=[pltpu.VMEM((n_heads, LANES), F32)],
        compiler_params=_cparams(("arbitrary",)),
        name="cumsum_logf",
    )(lf)


def _idx_prompt_kernel(ki_ref, qt_ref, w_ref, mask_ref, ks_ref, *, n_sel, tq, n_heads, cb):
    i = pl.program_id(0)
    n_blocks = pl.num_programs(0)
    big = cb * tq

    @pl.when(i == 0)
    def _():
        ks_ref[...] = jnp.full(ks_ref.shape, NEG_INF_KEY, I32)

    w = w_ref[0]
    row = lax.broadcasted_iota(I32, (tq, tq), 0)
    col = lax.broadcasted_iota(I32, (tq, tq), 1)
    hpd = 2

    def score_chunk(c):
        r0 = pl.multiple_of(c * tq, tq)
        kc = ki_ref[pl.ds(r0, tq), :]
        acc = jnp.zeros((tq, tq), F32)
        for g in range(n_heads // hpd):
            sc = jnp.dot(kc, qt_ref[0, :, g * hpd * tq:(g + 1) * hpd * tq], preferred_element_type=F32)
            for u in range(hpd):
                h = g * hpd + u
                acc = acc + jnp.maximum(sc[:, u * tq:(u + 1) * tq], 0.0) * w[h:h + 1, :]
        acc = jnp.where(row + (c - i) * tq <= col, acc, NEG_INF)
        ks_ref[pl.ds(r0, tq), :] = _sort_key(acc)

    def score_pair(pr, carry):
        score_chunk(2 * pr)
        score_chunk(2 * pr + 1)
        return carry

    lax.fori_loop(0, (i + 2) // 2, score_pair, 0)

    n_big = (i + cb) // cb

    def count_ge(cand):
        def body(c, acc):
            r0 = pl.multiple_of(c * big, big)
            hit = jnp.where(ks_ref[pl.ds(r0, big), :] >= cand, 1, 0).astype(I32)
            return acc + hit.reshape(big // SUBLANES, SUBLANES, tq).sum(axis=0)
        acc = lax.fori_loop(0, n_big, body, jnp.zeros((SUBLANES, tq), I32))
        return jnp.sum(acc, axis=0, keepdims=True)

    def bisect(b, ans_u):
        cand_u = ans_u | jnp.left_shift(jnp.int32(1), 31 - b)
        cnt = count_ge(cand_u ^ jnp.int32(INT_MIN))
        return jnp.where(cnt >= n_sel, cand_u, ans_u)

    ans_u = lax.fori_loop(0, 32, bisect, jnp.zeros((1, tq), I32))
    thr = ans_u ^ jnp.int32(INT_MIN)
    tied = jnp.max(jnp.abs(count_ge(thr) - n_sel)) > 0

    @pl.when(jnp.logical_not(tied))
    def _():
        def finish(c, carry):
            r0 = pl.multiple_of(c * tq, tq)
            key = ks_ref[pl.ds(r0, tq), :]
            mask_ref[0, pl.ds(r0, tq), :] =jnp.where(key >= thr, 0.0, NEG_INF).astype(BF16)
            return carry

        lax.fori_loop(0, i + 1, finish, 0)

    @pl.when(tied)
    def _():
        need = (n_sel - count_ge(thr + 1)).astype(F32)
        strict_lower = jnp.where(col < row, 1.0, 0.0).astype(BF16)

        def finish(c, seen):
            r0 = pl.multiple_of(c * tq, tq)
            key = ks_ref[pl.ds(r0, tq), :]
            eq = jnp.where(key == thr, 1.0, 0.0)
            before = jnp.dot(strict_lower, eq.astype(BF16), preferred_element_type=F32) + seen
            tie_ok = jnp.where(before < need, eq, 0.0)
            sel = jnp.where(key > thr, 1.0, tie_ok)
            mask_ref[0, pl.ds(r0, tq), :] =jnp.where(sel > 0.0, 0.0, NEG_INF).astype(BF16)
            return seen + jnp.sum(eq, axis=0, keepdims=True)

        lax.fori_loop(0, i + 1, finish, jnp.zeros((1, tq), F32))

    def fill(c, carry):
        r0 = pl.multiple_of(c * tq, tq)
        mask_ref[0, pl.ds(r0, tq), :] =jnp.full((tq, tq), NEG_INF, BF16)
        return carry

    lax.fori_loop(i + 1, n_blocks, fill, 0)


def _idx_prompt(ki, qt, w, *, n_sel, tq=128, cb=4):
    s, idim = ki.shape
    nb, n_heads, _ = w.shape
    assert nb * tq == s and (s // tq) % cb == 0 and cb * tq >= n_sel
    return pl.pallas_call(
        functools.partial(_idx_prompt_kernel, n_sel=n_sel, tq=tq, n_heads=n_heads, cb=cb),
        grid=(nb,),
        in_specs=[
            pl.BlockSpec((s, idim), lambda i: (0, 0)),
            pl.BlockSpec((1, idim, n_heads * tq), lambda i: (i, 0, 0)),
            pl.BlockSpec((1, n_heads, tq), lambda i: (i, 0, 0)),
        ],
        out_specs=pl.BlockSpec((1, s, tq), lambda i: (i, 0, 0)),
        out_shape=jax.ShapeDtypeStruct((nb, s, tq), BF16),
        scratch_shapes=[pltpu.VMEM((s, tq), I32)],
        compiler_params=_cparams(("arbitrary",)),
        name="idx_prompt",
    )(ki, qt, w)


def _attn_prompt_kernel(it_ref, jt_ref, qt_ref, k_ref, vt_ref, b1_ref, b2_ref, o_ref,
                        m_sc, l_sc, acc_sc, s_sc, mx_sc, *, mode, t, n_heads, hd, scale):
    s = pl.program_id(0)
    i = it_ref[s]
    j = jt_ref[s]

    @pl.when(j == 0)
    def _():
        m_sc[...] = jnp.full(m_sc.shape, M_INIT, F32)
        l_sc[...] = jnp.zeros_like(l_sc)
        acc_sc[...] = jnp.zeros_like(acc_sc)

    def logits(diagonal):
        if mode == "dsa":
            sel_bias = jnp.concatenate([b1_ref[u] for u in range(t // LANES)], axis=1).astype(F32)
            near = jnp.minimum(i - j, 2)
        for h in range(n_heads):
            hs = slice(h * hd, (h + 1) * hd)
            st = jnp.dot(k_ref[:, hs], qt_ref[0, hs, :], preferred_element_type=F32) * (scale * LOG2E)
            if mode == "fox":
                st = st - jnp.concatenate([b1_ref[h]] * (t // LANES), axis=1)
                if diagonal:
                    st = st + b2_ref[...]
            else:
                st = st + b2_ref[h, near] + sel_bias
            s_sc[h] = st
            mx_sc[h] = jnp.max(st, axis=0, keepdims=True)

    if mode == "fox":
        pl.when(j == i)(functools.partial(logits, True))
        pl.when(j != i)(functools.partial(logits, False))
    else:
        logits(None)

    for h in range(n_heads):
        hs = slice(h * hd, (h + 1) * hd)
        st = s_sc[h]
        m_old = m_sc[h]
        m_new = jnp.maximum(m_old, mx_sc[h])
        alpha = jnp.exp2(m_old - m_new)
        p = jnp.exp2(st - m_new)
        l_sc[h] = alpha * l_sc[h] + jnp.sum(p, axis=0, keepdims=True)
        pv = jnp.dot(vt_ref[0, hs, :], p.astype(BF16), preferred_element_type=F32)
        acc_sc[hs, :] = alpha * acc_sc[hs, :] + pv
        m_sc[h] = m_new

    @pl.when(j == i)
    def _():
        for h in range(n_heads):
            hs = slice(h * hd, (h + 1) * hd)
            o_ref[0, hs, :] = (acc_sc[hs, :] / l_sc[h]).astype(BF16)


def _attn_prompt(mode, qt, k, vt, b1, b2, *, t, n_heads, hd):
    nb, width, _ = qt.shape
    s = nb * t
    assert t % LANES == 0
    pairs = [(i, j) for i in range(nb) for j in range(i + 1)]
    it = jnp.asarray([p[0] for p in pairs], I32)
    jt = jnp.asarray([p[1] for p in pairs], I32)
    if mode == "fox":
        b1_spec = pl.BlockSpec((n_heads, t, LANES), lambda g, it, jt: (0, jt[g], 0))
    else:
        b1_spec = pl.BlockSpec((t // LANES, t, LANES), lambda g, it, jt: (it[g], jt[g], 0))
    grid_spec = pltpu.PrefetchScalarGridSpec(
        num_scalar_prefetch=2,
        grid=(len(pairs),),
        in_specs=[
            pl.BlockSpec((1, width, t), lambda g, it, jt: (it[g], 0, 0)),
            pl.BlockSpec((t, width), lambda g, it, jt: (jt[g], 0)),
            pl.BlockSpec((1, width, t), lambda g, it, jt: (jt[g], 0, 0)),
            b1_spec,
            pl.BlockSpec(memory_space=pltpu.VMEM),
        ],
        out_specs=pl.BlockSpec((1, width, t), lambda g, it, jt: (it[g], 0, 0)),
        scratch_shapes=[
            pltpu.VMEM((n_heads, 1, t), F32),
            pltpu.VMEM((n_heads, 1, t), F32),
            pltpu.VMEM((width, t), F32),
            pltpu.VMEM((n_heads, t, t), F32),
            pltpu.VMEM((n_heads, 1, t), F32),
        ],
    )
    return pl.pallas_call(
        functools.partial(_attn_prompt_kernel, mode=mode, t=t, n_heads=n_heads, hd=hd,
                          scale=hd ** -0.5),
        grid_spec=grid_spec,
        out_shape=jax.ShapeDtypeStruct((nb, width, t), BF16),
        compiler_params=_cparams(("arbitrary",)),
        name="attn_prompt_" + mode,
    )(it, jt, qt, k, vt, b1, b2)


def _idx_sample_kernel(pt_ref, *refs, n_sel, pps, page, n_steps, t_new, n_heads, group):
    page_refs = refs[:pps]
    qi_ref, w_ref, kn_ref, mask_ref, sc_sc = refs[pps:]
    st = pl.program_id(1)
    span = pps * page
    width = sc_sc.shape[1]
    n_past = n_steps * span
    qi = qi_ref[0]
    w = w_ref[0]

    def scores(kb):
        sc = lax.dot_general(qi, kb, (((1,), (1,)), ((), ())), preferred_element_type=F32)
        sc = jnp.maximum(sc, 0.0) * jnp.concatenate([w] * (kb.shape[0] // LANES), axis=1)
        acc = sc[0:SUBLANES, :]
        for h in range(1, n_heads):
            acc = acc + sc[h * SUBLANES:(h + 1) * SUBLANES, :]
        return acc

    slot = pl.program_id(0) % group
    r0 = pl.multiple_of(slot * SUBLANES, SUBLANES)
    kb = jnp.concatenate([r[0].astype(BF16) for r in page_refs], axis=0)
    c0 = pl.multiple_of(st * span, span)
    sc_sc[pl.ds(r0, SUBLANES), pl.ds(c0, span)] = _sort_key(scores(kb))

    @pl.when(st == n_steps - 1)
    def _():
        qrow = lax.broadcasted_iota(I32, (SUBLANES, page), 0)
        jcol = lax.broadcasted_iota(I32, (SUBLANES, page), 1)
        new = jnp.where((jcol <= qrow) & (jcol < t_new), scores(kn_ref[0]), NEG_INF)
        sc_sc[pl.ds(r0, SUBLANES), n_past:n_past + page] = _sort_key(new)

    @pl.when((st == n_steps - 1) & (slot == group - 1))
    def _():
        rows = group * SUBLANES

        def bisect(b, ans_u):
            cand_u = ans_u | jnp.left_shift(jnp.int32(1), 31 - b)
            cand = cand_u ^ jnp.int32(INT_MIN)
            cnt = jnp.sum(jnp.where(sc_sc[...] >= cand, 1, 0).astype(I32), axis=1, keepdims=True)
            return jnp.where(cnt >= n_sel, cand_u, ans_u)

        ans_u = lax.fori_loop(0, 32, bisect, jnp.zeros((rows, 1), I32))
        thr = ans_u ^ jnp.int32(INT_MIN)
        n_gt = jnp.sum(jnp.where(sc_sc[...] > thr, 1, 0).astype(I32), axis=1, keepdims=True)
        need = (n_sel - n_gt).astype(F32)

        r = lax.broadcasted_iota(I32, (LANES, LANES), 0)
        c = lax.broadcasted_iota(I32, (LANES, LANES), 1)
        strict_upper = jnp.where(r < c, 1.0, 0.0).astype(BF16)
        seen = jnp.zeros((rows, 1), F32)
        for blk in range(width // LANES):
            cols = slice(blk * LANES, (blk + 1) * LANES)
            kblk = sc_sc[:, cols]
            eq = jnp.where(kblk == thr, 1.0, 0.0)
            before = jnp.dot(eq.astype(BF16), strict_upper, preferred_element_type=F32) + seen
            tie_ok = jnp.where(before < need, eq, 0.0)
            sel = jnp.where(kblk > thr, 1.0, tie_ok)
            mask_ref[:, :, cols] = jnp.where(sel > 0.0, 0.0, NEG_INF).reshape(group, SUBLANES, LANES)
            seen = seen + jnp.sum(eq, axis=1, keepdims=True)


def _idx_sample(pt_flat, pool_ik, qi, w, kn, *, n_sel, n_pages, t_new, n_heads, pps=8, group=8):
    b = qi.shape[0]
    page, idim = pool_ik.shape[1:]
    pps = math.gcd(pps, n_pages)
    group = math.gcd(group, b)
    assert page == LANES
    n_steps = n_pages // pps
    width = n_pages * page + page

    def page_spec(r):
        return pl.BlockSpec((1, page, idim),
                            lambda bi, st, pt: (pt[bi * n_pages + st * pps + r], 0, 0))

    grid_spec = pltpu.PrefetchScalarGridSpec(
        num_scalar_prefetch=1,
        grid=(b, n_steps),
        in_specs=[page_spec(r) for r in range(pps)] + [
            pl.BlockSpec((1, n_heads * SUBLANES, idim), lambda bi, st, pt: (bi, 0, 0)),
            pl.BlockSpec((1, n_heads * SUBLANES, LANES), lambda bi, st, pt: (bi, 0, 0)),
            pl.BlockSpec((1, page, idim), lambda bi, st, pt: (bi, 0, 0)),
        ],
        out_specs=pl.BlockSpec((group, SUBLANES, width), lambda bi, st, pt: (bi // group, 0, 0)),
        scratch_shapes=[pltpu.VMEM((group * SUBLANES, width), I32)],
    )
    return pl.pallas_call(
        functools.partial(_idx_sample_kernel, n_sel=n_sel, pps=pps, page=page, n_steps=n_steps,
                          t_new=t_new, n_heads=n_heads, group=group),
        grid_spec=grid_spec,
        out_shape=jax.ShapeDtypeStruct((b, SUBLANES, width), F32),
        compiler_params=_cparams(("arbitrary", "arbitrary")),
        name="idx_sample",
    )(pt_flat, *([pool_ik] * pps), qi, w, kn)


def _attn_sample_kernel(pt_ref, *refs, mode, pps, page, n_steps, t_new, n_heads, hd, scale):
    k_refs = refs[:pps]
    v_refs = refs[pps:2 * pps]
    rest = refs[2 * pps:]
    if mode == "fox":
        lf_refs = rest[:pps]
        q_ref, kn_ref, vn_ref, lfn_ref, o_ref, m_sc, l_sc, acc_sc, carry_sc = rest[pps:]
    else:
        q_ref, kn_ref, vn_ref, mask_ref, t5_ref, o_ref, m_sc, l_sc, acc_sc = rest
    st = pl.program_id(1)
    rows = SUBLANES * n_heads
    n_past = n_steps * pps * page

    @pl.when(st == 0)
    def _():
        m_sc[...] = jnp.full(m_sc.shape, M_INIT, F32)
        l_sc[...] = jnp.zeros_like(l_sc)
        acc_sc[...] = jnp.zeros_like(acc_sc)
        if mode == "fox":
            carry_sc[...] = jnp.zeros_like(carry_sc)

    span = pps * page
    is_last = st == n_steps - 1

    if mode == "fox":
        ru = lax.broadcasted_iota(I32, (page, page), 0)
        cu = lax.broadcasted_iota(I32, (page, page), 1)
        lower_incl = jnp.where(cu <= ru, 1.0, 0.0).astype(BF16)
        blocks = []
        before = jnp.zeros((1, 3 * n_heads), F32)
        for r in list(lf_refs) + [lfn_ref]:
            c = jnp.dot(lower_incl, jnp.concatenate(_split3(r[0]), axis=1), preferred_element_type=F32)
            blocks.append(c + before)
            before = before + c[page - 1:page, :]
        cum_parts = jnp.concatenate(_split3(jnp.concatenate(blocks, axis=0)), axis=1)
        rh = lax.broadcasted_iota(I32, (rows, 9 * n_heads), 0) // SUBLANES
        ch = lax.broadcasted_iota(I32, (rows, 9 * n_heads), 1) % n_heads
        head_of_row = jnp.where(rh == ch, 1.0, 0.0).astype(BF16)
        cum_t = lax.dot_general(head_of_row, cum_parts, (((1,), (1,)), ((), ())),
                                preferred_element_type=F32)
        f_cum = cum_t + carry_sc[...]
        carry_sc[...] = f_cum[:, span - 1:span]
        qrow = lax.broadcasted_iota(I32, (rows, page), 0) % SUBLANES
        jcol = lax.broadcasted_iota(I32, (rows, page), 1)
        new_ok = (jcol <= qrow) & (jcol < t_new) & is_last
        bias = jnp.concatenate([-f_cum[:, :span], jnp.where(new_ok, -f_cum[:, span:], NEG_INF)], axis=1)
    else:
        def mask_rows(mk):
            return jnp.concatenate([mk] * n_heads, axis=0)

        c0 = pl.multiple_of(st * span, span)
        past_bias = t5_ref[:, pl.ds(c0, span)] + mask_rows(mask_ref[0, :, pl.ds(c0, span)])
        new_bias = t5_ref[:, n_past:n_past + page] + mask_rows(mask_ref[0, :, n_past:n_past + page])
        bias = jnp.concatenate([past_bias, jnp.where(is_last, new_bias, NEG_INF)], axis=1)

    def head_rows(page_refs, new_ref, h):
        return jnp.concatenate(
            [r[0, pl.ds(h, page, stride=n_heads), :].astype(BF16) for r in page_refs] + [new_ref[0, h]], axis=0)

    s = jnp.concatenate(
        [lax.dot_general(q_ref[0, h * SUBLANES:(h + 1) * SUBLANES, :], head_rows(k_refs, kn_ref, h),
                         (((1,), (1,)), ((), ())), preferred_element_type=F32) for h in range(n_heads)], axis=0)
    s = s * scale + bias
    m_old = m_sc[...]
    m_new = jnp.maximum(m_old, jnp.max(s, axis=1, keepdims=True))
    alpha = jnp.exp(m_old - m_new)
    p = jnp.exp(s - m_new)
    l_sc[...] = alpha * l_sc[...] + jnp.sum(p, axis=1, keepdims=True)
    p16 = p.astype(BF16)
    pv = jnp.concatenate(
        [jnp.dot(p16[h * SUBLANES:(h + 1) * SUBLANES, :], head_rows(v_refs, vn_ref, h),
                 preferred_element_type=F32) for h in range(n_heads)], axis=0)
    acc_sc[...] = alpha * acc_sc[...] + pv
    m_sc[...] = m_new

    @pl.when(is_last)
    def _():
        o_ref[0] = (acc_sc[...] / l_sc[...]).astype(BF16)


def _attn_sample(mode, pt_flat, pool_k, pool_v, q, kn, vn, extra, *, n_pages, t_new, n_heads, hd, pps=8):
    b, rows, _ = q.shape
    page = pool_k.shape[1] // n_heads
    pps = math.gcd(pps, n_pages)
    assert page == LANES and rows == SUBLANES * n_heads and t_new <= SUBLANES
    n_steps = n_pages // pps

    def page_spec(shape_tail, r):
        zeros = (0,) * len(shape_tail)
        return pl.BlockSpec((1,) + shape_tail,
                            lambda bi, st, pt: (pt[bi * n_pages + st * pps + r],) + zeros)

    def per_req(shape_tail):
        zeros = (0,) * len(shape_tail)
        return pl.BlockSpec((1,) + shape_tail, lambda bi, st, pt: (bi,) + zeros)

    in_specs = [page_spec((page * n_heads, hd), r) for r in range(pps)] * 2
    args = [pool_k] * pps + [pool_v] * pps
    scratch = [pltpu.VMEM((rows, 1), F32), pltpu.VMEM((rows, 1), F32), pltpu.VMEM((rows, hd), F32)]
    shared = [per_req((rows, hd)), per_req((n_heads, page, hd)), per_req((n_heads, page, hd))]
    if mode == "fox":
        pool_lf, lfn = extra
        in_specs += [page_spec((page, n_heads), r) for r in range(pps)]
        args += [pool_lf] * pps
        in_specs += shared + [per_req((page, n_heads))]
        args += [q, kn, vn, lfn]
        scratch.append(pltpu.VMEM((rows, 1), F32))
    else:
        mask, t5 = extra
        in_specs += shared + [per_req(mask.shape[1:]), pl.BlockSpec(memory_space=pltpu.VMEM)]
        args += [q, kn, vn, mask, t5]
    grid_spec = pltpu.PrefetchScalarGridSpec(
        num_scalar_prefetch=1,
        grid=(b, n_steps),
        in_specs=in_specs,
        out_specs=pl.BlockSpec((1, rows, hd), lambda bi, st, pt: (bi, 0, 0)),
        scratch_shapes=scratch,
    )
    return pl.pallas_call(
        functools.partial(_attn_sample_kernel, mode=mode, pps=pps, page=page, n_steps=n_steps,
                          t_new=t_new, n_heads=n_heads, hd=hd, scale=hd ** -0.5),
        grid_spec=grid_spec,
        out_shape=jax.ShapeDtypeStruct((b, rows, hd), BF16),
        compiler_params=_cparams(("arbitrary", "arbitrary")),
        name="attn_sample_" + mode,
    )(pt_flat, *args)


def _t5_bucket(dist, num_buckets):
    n = jnp.maximum(dist, 0)
    exact = num_buckets // 2
    nf = jnp.maximum(n, 1).astype(F32)
    large = exact + (jnp.log(nf / exact) / math.log(MAX_DISTANCE / exact)
                     * (num_buckets - exact)).astype(I32)
    return jnp.where(n < exact, n, jnp.minimum(large, num_buckets - 1))


def _t5_bias(rel_bias, dist):
    nb = rel_bias.shape[0]
    onehot = (_t5_bucket(dist, nb)[..., None] == jnp.arange(nb)).astype(F32)
    b = jnp.einsum("...n,nh->h...", onehot, rel_bias.astype(F32), precision=lax.Precision.HIGHEST)
    return jnp.where(dist[None] >= 0, b, NEG_INF)


def _is_pow2(x):
    return math.frexp(x)[0] == 0.5


def _mix_prompt(q_a, k_a, v_a, lf_a, q_b, k_b, v_b, q_i, k_i, w_i, rel_bias, *, n_heads, hd, t=256):
    s = q_a.shape[0]
    width = n_heads * hd
    to_blocks = lambda a: a.reshape(s // t, t, width).transpose(0, 2, 1)
    from_blocks = lambda a: a.transpose(0, 2, 1).reshape(s, width)
    idim = k_i.shape[1]
    ih = w_i.shape[1]
    n_sel = min(TOPK_MAX, s // 4)
    assert MAX_DISTANCE <= t + 1
    idx_scale, idx_w_scale = idim ** -0.5, ih ** -0.5
    assert _is_pow2(idx_scale) and _is_pow2(idx_w_scale)

    fk = _cumsum_rep(lf_a, out_scale=LOG2E)
    r = jnp.arange(t)
    dist0 = r[None, :] - r[:, None]
    causal = jnp.where(dist0 >= 0, 0.0, NEG_INF).astype(F32)
    o_a = from_blocks(_attn_prompt("fox", to_blocks(q_a), k_a, to_blocks(v_a), fk, causal,
                                   t=t, n_heads=n_heads, hd=hd))

    tq = LANES
    nb = s // tq
    qt = q_i.reshape(nb, tq, ih, idim).transpose(0, 3, 2, 1).reshape(nb, idim, ih * tq)
    w = (w_i * (idx_scale * idx_w_scale)).reshape(nb, tq, ih).transpose(0, 2, 1)
    mask = _idx_prompt(k_i, qt, w, n_sel=n_sel, tq=tq)
    btiles = jnp.stack([_t5_bias(rel_bias, dist0 + nd * t) for nd in range(3)], axis=1) * LOG2E
    o_b = from_blocks(_attn_prompt("dsa", to_blocks(q_b), k_b, to_blocks(v_b), mask, btiles,
                                   t=t, n_heads=n_heads, hd=hd))
    return o_a, o_b


def _head_major(x, n_heads, hd, t_pad):
    b, t_new, _ = x.shape
    xh = x.reshape(b, t_new, n_heads, hd).transpose(0, 2, 1, 3)
    return jnp.pad(xh, ((0, 0), (0, 0), (0, t_pad - t_new), (0, 0)))


def _pad_rows(x, rows):
    return jnp.pad(x, ((0, 0), (0, rows - x.shape[1]), (0, 0)))


def _mix_sample(q_a, k_a, v_a, lf_a, q_b, k_b, v_b, q_i, k_i, w_i, rel_bias, page_table,
                fox_k, fox_v, fox_lf, dsa_k, dsa_v, idx_k, *, layer, n_heads, hd):
    b, t_new, width = q_a.shape
    n_pages = page_table.shape[1]
    depth, n_pool, page = fox_k.shape[:3]
    idim = k_i.shape[2]
    ih = w_i.shape[2]
    past = n_pages * page
    n_sel = min(TOPK_MAX, (past + t_new) // 4)
    idx_scale, idx_w_scale = idim ** -0.5, ih ** -0.5
    assert _is_pow2(idx_scale) and _is_pow2(idx_w_scale)
    assert t_new <= SUBLANES
    pt_flat = page_table.reshape(-1).astype(I32) + layer * n_pool
    n_pool = depth * n_pool
    fox_lf = fox_lf.reshape(n_pool, page, n_heads)
    idx_k = idx_k.reshape(n_pool, page, idim)

    flat = lambda pool: pool.reshape(n_pool, page * n_heads, hd)
    q_rows = lambda q: _head_major(q, n_heads, hd, SUBLANES).reshape(b, n_heads * SUBLANES, hd)
    new_keys = lambda x: _head_major(x, n_heads, hd, page)
    o_a = _attn_sample("fox", pt_flat, flat(fox_k), flat(fox_v), q_rows(q_a), new_keys(k_a), new_keys(v_a),
                       (fox_lf, _pad_rows(lf_a, page)), n_pages=n_pages, t_new=t_new, n_heads=n_heads, hd=hd)

    qi = jnp.pad(q_i.reshape(b, t_new, ih, idim), ((0, 0), (0, SUBLANES - t_new), (0, 0), (0, 0)))
    qi = qi.transpose(0, 2, 1, 3).reshape(b, ih * SUBLANES, idim)
    wr = jnp.pad(w_i * (idx_scale * idx_w_scale), ((0, 0), (0, SUBLANES - t_new), (0, 0)))
    wr = jnp.broadcast_to(wr.transpose(0, 2, 1).reshape(b, ih * SUBLANES, 1), (b, ih * SUBLANES, LANES))
    mask = _idx_sample(pt_flat, idx_k, qi, wr, _pad_rows(k_i, page), n_sel=n_sel, n_pages=n_pages,
                       t_new=t_new, n_heads=ih)
    kpos = jnp.arange(past + page)
    qpos = past + jnp.arange(t_new)
    dist = jnp.where(kpos[None, :] < past + t_new, qpos[:, None] - kpos[None, :], -1)
    t5 = _t5_bias(rel_bias, dist)
    t5 = jnp.pad(t5, ((0, 0), (0, SUBLANES - t_new), (0, 0))).reshape(n_heads * SUBLANES, past + page)
    o_b = _attn_sample("dsa", pt_flat, flat(dsa_k), flat(dsa_v), q_rows(q_b), new_keys(k_b), new_keys(v_b),
                       (mask, t5), n_pages=n_pages, t_new=t_new, n_heads=n_heads, hd=hd)
    tokens = lambda o: o.reshape(b, n_heads, SUBLANES, hd)[:, :, :t_new].transpose(0, 2, 1, 3).reshape(b * t_new, width)
    return tokens(o_a), tokens(o_b)


def kernel(x_prompt, x_sample, cache_fox_k, cache_fox_v, cache_fox_logf, cache_dsa_k, cache_dsa_v, cache_idx_k, page_table, p_prompt, p_sample, rel_bias, g_ffn1, w1_pre, w3_pre, w2_pre, g_mix, w_in, b_forget, w_branch_fox, w_branch_dsa, w_out, g_ffn2, w1_post, w3_post, w2_post, g_ple, w_ple, w_ple_gate, g_final):
    depth = w_in.shape[0]
    bp, sp, d = x_prompt.shape
    bs, ts, _ = x_sample.shape
    assert bp == 1
    n_fox, hd = cache_fox_k.shape[3:]
    n_dsa = cache_dsa_k.shape[3]
    assert n_fox == n_dsa
    idim = cache_idx_k.shape[-1]
    ih = IDX_HEADS
    fw, dw = n_fox * hd, n_dsa * hd
    n_in = w_in.shape[2]
    assert n_in == 3 * fw + n_fox + 3 * dw + ih * idim + idim + ih + 2 * d
    mp, ms = bp * sp, bs * ts
    small_w = n_fox + idim + ih
    assert small_w <= LANES

    x = jnp.concatenate([x_prompt.reshape(mp, d), x_sample.reshape(ms, d)], axis=0)
    st_p, st_s = [], []
    for l in range(depth):
        offs = np.cumsum([0, fw, fw, fw, n_fox, dw, dw, dw, ih * idim, idim, ih, 2 * d])
        col = lambda a, c: w_in[l][:, offs[a]:offs[a] + c]
        w_q = jnp.concatenate([col(0, fw), col(4, dw), col(7, ih * idim)], axis=1).astype(BF16)
        w_kv = [col(c, fw).astype(BF16) for c in (1, 2, 5, 6)]
        w_gate = col(10, 2 * d).astype(BF16)
        w_small = jnp.concatenate([col(3, n_fox), col(8, idim), col(9, ih)], axis=1)
        w_small = jnp.pad(w_small, ((0, 0), (0, LANES - small_w))).astype(BF16)
        b_small = jnp.pad(b_forget[l], (0, LANES - n_fox)).reshape(1, LANES)

        x1, h_mix = _ffn(x, g_ffn1[l], w1_pre[l].astype(BF16), w3_pre[l].astype(BF16),
                         w2_pre[l].astype(BF16), g_mix[l])
        zq = _proj("q", h_mix, w_q)
        kv_p, kv_s, kv16 = zip(*[_proj_kv(h_mix, w, m_prompt=mp, n_heads=n_fox, hd=hd) for w in w_kv])
        gate = _proj("gate", h_mix, w_gate)
        small = _proj("small", h_mix, w_small, bias=b_small, n_fox=n_fox)

        q_a, q_b, q_i = zq[:, :fw], zq[:, fw:fw + dw], zq[:, fw + dw:]
        lf = small[:, :n_fox]
        k_i32 = small[:, n_fox:n_fox + idim]
        w_i = small[:, n_fox + idim:small_w]
        k_i16 = k_i32.astype(BF16)

        o_ap, o_bp = _mix_prompt(q_a[:mp], kv16[0], kv16[1][:mp], lf[:mp], q_b[:mp], kv16[2],
                                 kv16[3][:mp], q_i[:mp], k_i16[:mp], w_i[:mp], rel_bias,
                                 n_heads=n_fox, hd=hd)
        rs = lambda a: a[mp:].reshape(bs, ts, -1)
        o_as, o_bs = _mix_sample(rs(q_a), rs(kv16[0]), rs(kv16[1]), rs(lf), rs(q_b), rs(kv16[2]),
                                 rs(kv16[3]), rs(q_i), rs(k_i16), rs(w_i), rel_bias, page_table,
                                 cache_fox_k, cache_fox_v, cache_fox_logf, cache_dsa_k,
                                 cache_dsa_v, cache_idx_k, layer=l, n_heads=n_fox, hd=hd)
        o_a = jnp.concatenate([o_ap, o_as], axis=0)
        o_b = jnp.concatenate([o_bp, o_bs], axis=0)

        merged = _merge(o_a, o_b, gate, w_branch_fox[l].astype(BF16), w_branch_dsa[l].astype(BF16))
        x2 = _resid_mm(x1, merged, w_out[l].astype(BF16))
        x3, h_ple = _ffn(x2, g_ffn2[l], w1_post[l].astype(BF16), w3_post[l].astype(BF16),
                         w2_post[l].astype(BF16), g_ple[l])
        p = jnp.concatenate([p_prompt[l].reshape(mp, -1), p_sample[l].reshape(ms, -1)], axis=0).astype(BF16)
        ple_w = (w_ple[l].astype(BF16), w_ple_gate[l].astype(BF16), g_final)
        if l + 1 < depth:
            x, = _ple(x3, p, h_ple, *ple_w)
        else:
            y_p, y_s = _ple(x3, p, h_ple, *ple_w, m_prompt=mp)

        heads = lambda a, bb, tt: a.reshape(bb, tt, n_fox, hd)
        st_p.append((heads(kv_p[0], bp, sp), heads(kv_p[1], bp, sp), lf[:mp].reshape(bp, sp, n_fox),
                     heads(kv_p[2], bp, sp), heads(kv_p[3], bp, sp), k_i32[:mp].reshape(bp, sp, idim)))
        st_s.append((heads(kv_s[0], bs, ts), heads(kv_s[1], bs, ts), lf[mp:].reshape(bs, ts, n_fox),
                     heads(kv_s[2], bs, ts), heads(kv_s[3], bs, ts), k_i32[mp:].reshape(bs, ts, idim)))

    pfk, pfv, pflf, pdk, pdv, pik = [jnp.stack(a) for a in zip(*st_p)]
    sfk, sfv, sflf, sdk, sdv, sik = [jnp.stack(a) for a in zip(*st_s)]
    y_prompt = y_p.reshape(bp, sp, d)
    y_sample = y_s.reshape(bs, ts, d)
    return (y_prompt, y_sample, pfk, pfv, pflf, pdk, pdv, pik, sfk, sfv, sflf, sdk, sdv, sik)
```

```python
import functools
import math

import numpy as np
import jax
import jax.numpy as jnp
from jax import lax
from jax.experimental import pallas as pl
from jax.experimental.pallas import tpu as pltpu

F32 = jnp.float32
BF16 = jnp.bfloat16
I32 = jnp.int32

TOPK_MAX = 256
MAX_DISTANCE = 128
RMS_EPS = 1e-6
IDX_HEADS = 16

V7X_VMEM_BYTES = 64 * 1024 * 1024
VMEM_LIMIT = V7X_VMEM_BYTES - 8 * 1024 * 1024
LANES = 128
SUBLANES = 8

LOG2E = math.log2(math.e)
NEG_INF = float("-inf")
M_INIT = -1e30
INT_MIN = -2 ** 31
NEG_INF_KEY = int(np.int32(np.uint32(0xFF800000) ^ np.uint32(0x7FFFFFFF)))


def _cparams(sem):
    return pltpu.CompilerParams(dimension_semantics=sem, vmem_limit_bytes=VMEM_LIMIT)


def _rms(x, g):
    ms = jnp.mean(x * x, axis=-1, keepdims=True)
    return x * lax.rsqrt(ms + RMS_EPS) * g


def _sigmoid(x):
    return 1.0 / (1.0 + jnp.exp(-x))


def _split3(x):
    hi = x.astype(BF16)
    r1 = x - hi.astype(F32)
    mid = r1.astype(BF16)
    lo = (r1 - mid.astype(F32)).astype(BF16)
    return hi, mid, lo


def _sort_key(x):
    bits = pltpu.bitcast(x, I32)
    return bits ^ ((bits >> 31) & jnp.int32(0x7FFFFFFF))


def _ffn_kernel(x_ref, g_ref, w1_ref, w3_ref, w2_ref, gn_ref, o_ref, hn_ref, h_sc, acc_sc):
    f = pl.program_id(1)

    @pl.when(f == 0)
    def _():
        h_sc[...] = _rms(x_ref[...], g_ref[...]).astype(BF16)
        acc_sc[...] = jnp.zeros_like(acc_sc)

    h = h_sc[...]
    a = jnp.dot(h, w1_ref[...], preferred_element_type=F32)
    b = jnp.dot(h, w3_ref[...], preferred_element_type=F32)
    act = (a * _sigmoid(a) * b).astype(BF16)
    acc_sc[...] += jnp.dot(act, w2_ref[...], preferred_element_type=F32)

    @pl.when(f == pl.num_programs(1) - 1)
    def _():
        x1 = x_ref[...] + 0.5 * acc_sc[...]
        o_ref[...] = x1
        hn_ref[...] = _rms(x1, gn_ref[...]).astype(BF16)


def _ffn(x, g, w1, w3, w2, g_next, *, tm=512, tf=512):
    m, d = x.shape
    dff = w1.shape[1]
    assert m % tm == 0 and dff % tf == 0
    return pl.pallas_call(
        _ffn_kernel,
        grid=(m // tm, dff // tf),
        in_specs=[
            pl.BlockSpec((tm, d), lambda i, f: (i, 0)),
            pl.BlockSpec((1, d), lambda i, f: (0, 0)),
            pl.BlockSpec((d, tf), lambda i, f: (0, f)),
            pl.BlockSpec((d, tf), lambda i, f: (0, f)),
            pl.BlockSpec((tf, d), lambda i, f: (f, 0)),
            pl.BlockSpec((1, d), lambda i, f: (0, 0)),
        ],
        out_specs=[
            pl.BlockSpec((tm, d), lambda i, f: (i, 0)),
            pl.BlockSpec((tm, d), lambda i, f: (i, 0)),
        ],
        out_shape=[jax.ShapeDtypeStruct((m, d), F32), jax.ShapeDtypeStruct((m, d), BF16)],
        scratch_shapes=[pltpu.VMEM((tm, d), BF16), pltpu.VMEM((tm, d), F32)],
        compiler_params=_cparams(("parallel", "arbitrary")),
        name="ffn",
    )(x, g.reshape(1, d), w1, w3, w2, g_next.reshape(1, d))


def _proj_q_kernel(h_ref, w_ref, o16_ref):
    o16_ref[...] = jnp.dot(h_ref[...], w_ref[...], preferred_element_type=F32).astype(BF16)


def _proj_kv_kernel(h_ref, w_ref, op_ref, os_ref, o16_ref, *, n_heads, hd, prompt_tiles):
    i = pl.program_id(0)
    z = jnp.dot(h_ref[...], w_ref[...], preferred_element_type=F32)
    o16_ref[...] = z.astype(BF16)

    def put(dst):
        for h in range(n_heads):
            dst[:, h, :] = z[:, h * hd:(h + 1) * hd]

    pl.when(i < prompt_tiles)(functools.partial(put, op_ref))
    pl.when(i >= prompt_tiles)(functools.partial(put, os_ref))


def _proj_kv(h, w, *, m_prompt, n_heads, hd, tm=512):
    m, k = h.shape
    n = w.shape[1]
    m_sample = m - m_prompt
    assert n == n_heads * hd and m_prompt % tm == 0 and m_sample % tm == 0
    pt = m_prompt // tm
    return pl.pallas_call(
        functools.partial(_proj_kv_kernel, n_heads=n_heads, hd=hd, prompt_tiles=pt),
        grid=(m // tm,),
        in_specs=[pl.BlockSpec((tm, k), lambda i: (i, 0)), pl.BlockSpec((k, n), lambda i: (0, 0))],
        out_specs=[
            pl.BlockSpec((tm, n_heads, hd), lambda i: (jnp.minimum(i, pt - 1), 0, 0)),
            pl.BlockSpec((tm, n_heads, hd), lambda i: (jnp.maximum(i - pt, 0), 0, 0)),
            pl.BlockSpec((tm, n), lambda i: (i, 0)),
        ],
        out_shape=[jax.ShapeDtypeStruct((m_prompt, n_heads, hd), F32),
                   jax.ShapeDtypeStruct((m_sample, n_heads, hd), F32),
                   jax.ShapeDtypeStruct((m, n), BF16)],
        compiler_params=_cparams(("arbitrary",)),
        name="proj_kv",
    )(h, w)


def _proj_gate_kernel(h_ref, w_ref, o_ref):
    z = jnp.dot(h_ref[...], w_ref[...], preferred_element_type=F32)
    o_ref[...] = _sigmoid(z)


def _proj_small_kernel(h_ref, w_ref, b_ref, o_ref, *, n_fox):
    z = jnp.dot(h_ref[...], w_ref[...], preferred_element_type=F32)
    lane = lax.broadcasted_iota(I32, z.shape, 1)
    zf = z + b_ref[...]
    logsig = jnp.minimum(zf, 0.0) - jnp.log1p(jnp.exp(-jnp.abs(zf)))
    o_ref[...] = jnp.where(lane < n_fox, logsig, z)


def _proj(kind, h, w, *, tm=512, tn=1024, bias=None, n_fox=0):
    m, k = h.shape
    n = w.shape[1]
    tn = min(tn, n)
    assert m % tm == 0 and n % tn == 0
    grid = (n // tn, m // tm)
    h_spec = pl.BlockSpec((tm, k), lambda j, i: (i, 0))
    w_spec = pl.BlockSpec((k, tn), lambda j, i: (0, j))
    o_spec = pl.BlockSpec((tm, tn), lambda j, i: (i, j))
    cp = _cparams(("parallel", "parallel"))
    if kind == "q":
        return pl.pallas_call(
            _proj_q_kernel, grid=grid, in_specs=[h_spec, w_spec], out_specs=o_spec,
            out_shape=jax.ShapeDtypeStruct((m, n), BF16), compiler_params=cp, name="proj_q")(h, w)
    if kind == "gate":
        return pl.pallas_call(
            _proj_gate_kernel, grid=grid, in_specs=[h_spec, w_spec], out_specs=o_spec,
            out_shape=jax.ShapeDtypeStruct((m, n), F32), compiler_params=cp, name="proj_gate")(h, w)
    assert kind == "small"
    b_spec = pl.BlockSpec((1, tn), lambda j, i: (0, j))
    return pl.pallas_call(
        functools.partial(_proj_small_kernel, n_fox=n_fox), grid=grid,
        in_specs=[h_spec, w_spec, b_spec], out_specs=o_spec,
        out_shape=jax.ShapeDtypeStruct((m, n), F32), compiler_params=cp, name="proj_small")(h, w, bias)


def _merge_kernel(oa_ref, ob_ref, g_ref, wa_ref, wb_ref, m_ref, *, d):
    ua = jnp.dot(oa_ref[...], wa_ref[...], preferred_element_type=F32)
    ub = jnp.dot(ob_ref[...], wb_ref[...], preferred_element_type=F32)
    m_ref[...] = (g_ref[:, :d] * ua + g_ref[:, d:] * ub).astype(BF16)


def _merge(oa, ob, gate, wa, wb, *, tm=256):
    m, wdt = oa.shape
    d = wa.shape[1]
    assert m % tm == 0
    return pl.pallas_call(
        functools.partial(_merge_kernel, d=d),
        grid=(m // tm,),
        in_specs=[
            pl.BlockSpec((tm, wdt), lambda i: (i, 0)),
            pl.BlockSpec((tm, wdt), lambda i: (i, 0)),
            pl.BlockSpec((tm, 2 * d), lambda i: (i, 0)),
            pl.BlockSpec((wdt, d), lambda i: (0, 0)),
            pl.BlockSpec((wdt, d), lambda i: (0, 0)),
        ],
        out_specs=pl.BlockSpec((tm, d), lambda i: (i, 0)),
        out_shape=jax.ShapeDtypeStruct((m, d), BF16),
        compiler_params=_cparams(("parallel",)),
        name="merge",
    )(oa, ob, gate, wa, wb)


def _resid_mm_kernel(x_ref, m_ref, w_ref, o_ref):
    o_ref[...] = x_ref[...] + jnp.dot(m_ref[...], w_ref[...], preferred_element_type=F32)


def _resid_mm(x, mm, w, *, tm=256):
    m, d = x.shape
    k = mm.shape[1]
    assert m % tm == 0
    return pl.pallas_call(
        _resid_mm_kernel,
        grid=(m // tm,),
        in_specs=[
            pl.BlockSpec((tm, d), lambda i: (i, 0)),
            pl.BlockSpec((tm, k), lambda i: (i, 0)),
            pl.BlockSpec((k, d), lambda i: (0, 0)),
        ],
        out_specs=pl.BlockSpec((tm, d), lambda i: (i, 0)),
        out_shape=jax.ShapeDtypeStruct((m, d), F32),
        compiler_params=_cparams(("parallel",)),
        name="resid_mm",
    )(x, mm, w)


def _ple_kernel(x_ref, p_ref, h_ref, wp_ref, wg_ref, gn_ref, *o_refs, prompt_tiles):
    e = jnp.dot(p_ref[...], wp_ref[...], preferred_element_type=F32)
    gt = jnp.dot(h_ref[...], wg_ref[...], preferred_element_type=F32)
    x4 = x_ref[...] + e * _sigmoid(gt)
    if prompt_tiles is None:
        o_refs[0][...] = x4
    else:
        i = pl.program_id(0)
        y = _rms(x4, gn_ref[...])

        @pl.when(i < prompt_tiles)
        def _():
            o_refs[0][...] = y

        @pl.when(i >= prompt_tiles)
        def _():
            o_refs[1][...] = y


def _ple(x, p, h, wp, wg, g_final, *, m_prompt=None, tm=256):
    m, d = x.shape
    pd = p.shape[1]
    assert m % tm == 0
    if m_prompt is None:
        pt = None
        out_specs = [pl.BlockSpec((tm, d), lambda i: (i, 0))]
        out_shape = [jax.ShapeDtypeStruct((m, d), F32)]
    else:
        assert m_prompt % tm == 0
        pt = m_prompt // tm
        out_specs = [pl.BlockSpec((tm, d), lambda i: (jnp.minimum(i, pt - 1), 0)),
                     pl.BlockSpec((tm, d), lambda i: (jnp.maximum(i - pt, 0), 0))]
        out_shape = [jax.ShapeDtypeStruct((m_prompt, d), F32), jax.ShapeDtypeStruct((m - m_prompt, d), F32)]
    return pl.pallas_call(
        functools.partial(_ple_kernel, prompt_tiles=pt),
        grid=(m // tm,),
        in_specs=[
            pl.BlockSpec((tm, d), lambda i: (i, 0)),
            pl.BlockSpec((tm, pd), lambda i: (i, 0)),
            pl.BlockSpec((tm, d), lambda i: (i, 0)),
            pl.BlockSpec((pd, d), lambda i: (0, 0)),
            pl.BlockSpec((d, d), lambda i: (0, 0)),
            pl.BlockSpec((1, d), lambda i: (0, 0)),
        ],
        out_specs=out_specs,
        out_shape=out_shape,
        compiler_params=_cparams(("arbitrary",)),
        name="ple",
    )(x, p, h, wp, wg, g_final.reshape(1, d))


def _cumsum_kernel(lf_ref, o_ref, carry_sc, *, n_heads, tb, out_scale):
    @pl.when(pl.program_id(0) == 0)
    def _():
        carry_sc[...] = jnp.zeros_like(carry_sc)

    row = lax.broadcasted_iota(I32, (tb, tb), 0)
    col = lax.broadcasted_iota(I32, (tb, tb), 1)
    tri = jnp.where(col <= row, 1.0, 0.0).astype(F32)
    lf = lf_ref[...]
    for h in range(n_heads):
        lfb = jnp.broadcast_to(lf[:, h:h + 1], (tb, LANES))
        cum = jnp.dot(tri, lfb, preferred_element_type=F32, precision=lax.Precision.HIGHEST)
        cum = cum + carry_sc[h:h + 1, :]
        o_ref[h] = cum * out_scale
        carry_sc[h:h + 1, :] = cum[tb - 1:tb, :]


def _cumsum_rep(lf, *, out_scale, tb=256):
    s, n_heads = lf.shape
    assert s % tb == 0
    return pl.pallas_call(
        functools.partial(_cumsum_kernel, n_heads=n_heads, tb=tb, out_scale=out_scale),
        grid=(s // tb,),
        in_specs=[pl.BlockSpec((tb, n_heads), lambda i: (i, 0))],
        out_specs=pl.BlockSpec((n_heads, tb, LANES), lambda i: (0, i, 0)),
        out_shape=jax.ShapeDtypeStruct((n_heads, s, LANES), F32),
        scratch_shapes=[pltpu.VMEM((n_heads, LANES), F32)],
        compiler_params=_cparams(("arbitrary",)),
        name="cumsum_logf",
    )(lf)


def _idx_prompt_kernel(ki_ref, qt_ref, w_ref, mask_ref, ks_ref, *, n_sel, tq, n_heads, cb):
    i = pl.program_id(0)
    n_blocks = pl.num_programs(0)
    big = cb * tq

    @pl.when(i == 0)
    def _():
        ks_ref[...] = jnp.full(ks_ref.shape, NEG_INF_KEY, I32)

    w = w_ref[0]
    row = lax.broadcasted_iota(I32, (tq, tq), 0)
    col = lax.broadcasted_iota(I32, (tq, tq), 1)
    hpd = 2

    def score_chunk(c):
        r0 = pl.multiple_of(c * tq, tq)
        kc = ki_ref[pl.ds(r0, tq), :]
        acc = jnp.zeros((tq, tq), F32)
        for g in range(n_heads // hpd):
            sc = jnp.dot(kc, qt_ref[0, :, g * hpd * tq:(g + 1) * hpd * tq], preferred_element_type=F32)
            for u in range(hpd):
                h = g * hpd + u
                acc = acc + jnp.maximum(sc[:, u * tq:(u + 1) * tq], 0.0) * w[h:h + 1, :]
        acc = jnp.where(row + (c - i) * tq <= col, acc, NEG_INF)
        ks_ref[pl.ds(r0, tq), :] = _sort_key(acc)

    def score_pair(pr, carry):
        score_chunk(2 * pr)
        score_chunk(2 * pr + 1)
        return carry

    lax.fori_loop(0, (i + 2) // 2, score_pair, 0)

    n_big = (i + cb) // cb

    def count_ge(cand):
        def body(c, acc):
            r0 = pl.multiple_of(c * big, big)
            hit = jnp.where(ks_ref[pl.ds(r0, big), :] >= cand, 1, 0).astype(I32)
            return acc + hit.reshape(big // SUBLANES, SUBLANES, tq).sum(axis=0)
        acc = lax.fori_loop(0, n_big, body, jnp.zeros((SUBLANES, tq), I32))
        return jnp.sum(acc, axis=0, keepdims=True)

    def bisect(b, ans_u):
        cand_u = ans_u | jnp.left_shift(jnp.int32(1), 31 - b)
        cnt = count_ge(cand_u ^ jnp.int32(INT_MIN))
        return jnp.where(cnt >= n_sel, cand_u, ans_u)

    ans_u = lax.fori_loop(0, 32, bisect, jnp.zeros((1, tq), I32))
    thr = ans_u ^ jnp.int32(INT_MIN)
    tied = jnp.max(jnp.abs(count_ge(thr) - n_sel)) > 0

    @pl.when(jnp.logical_not(tied))
    def _():
        def finish(c, carry):
            r0 = pl.multiple_of(c * tq, tq)
            key = ks_ref[pl.ds(r0, tq), :]
            mask_ref[0, pl.ds(r0, tq), :] =jnp.where(key >= thr, 0.0, NEG_INF).astype(BF16)
            return carry

        lax.fori_loop(0, i + 1, finish, 0)

    @pl.when(tied)
    def _():
        need = (n_sel - count_ge(thr + 1)).astype(F32)
        strict_lower = jnp.where(col < row, 1.0, 0.0).astype(BF16)

        def finish(c, seen):
            r0 = pl.multiple_of(c * tq, tq)
            key = ks_ref[pl.ds(r0, tq), :]
            eq = jnp.where(key == thr, 1.0, 0.0)
            before = jnp.dot(strict_lower, eq.astype(BF16), preferred_element_type=F32) + seen
            tie_ok = jnp.where(before < need, eq, 0.0)
            sel = jnp.where(key > thr, 1.0, tie_ok)
            mask_ref[0, pl.ds(r0, tq), :] =jnp.where(sel > 0.0, 0.0, NEG_INF).astype(BF16)
            return seen + jnp.sum(eq, axis=0, keepdims=True)

        lax.fori_loop(0, i + 1, finish, jnp.zeros((1, tq), F32))

    def fill(c, carry):
        r0 = pl.multiple_of(c * tq, tq)
        mask_ref[0, pl.ds(r0, tq), :] =jnp.full((tq, tq), NEG_INF, BF16)
        return carry

    lax.fori_loop(i + 1, n_blocks, fill, 0)


def _idx_prompt(ki, qt, w, *, n_sel, tq=128, cb=4):
    s, idim = ki.shape
    nb, n_heads, _ = w.shape
    assert nb * tq == s and (s // tq) % cb == 0 and cb * tq >= n_sel
    return pl.pallas_call(
        functools.partial(_idx_prompt_kernel, n_sel=n_sel, tq=tq, n_heads=n_heads, cb=cb),
        grid=(nb,),
        in_specs=[
            pl.BlockSpec((s, idim), lambda i: (0, 0)),
            pl.BlockSpec((1, idim, n_heads * tq), lambda i: (i, 0, 0)),
            pl.BlockSpec((1, n_heads, tq), lambda i: (i, 0, 0)),
        ],
        out_specs=pl.BlockSpec((1, s, tq), lambda i: (i, 0, 0)),
        out_shape=jax.ShapeDtypeStruct((nb, s, tq), BF16),
        scratch_shapes=[pltpu.VMEM((s, tq), I32)],
        compiler_params=_cparams(("arbitrary",)),
        name="idx_prompt",
    )(ki, qt, w)


def _attn_prompt_kernel(it_ref, jt_ref, qt_ref, k_ref, vt_ref, b1_ref, b2_ref, far_ref, o_ref,
                        m_sc, l_sc, acc_sc, s_sc, mx_sc, *, mode, t, n_heads, hd, scale):
    s = pl.program_id(0)
    i = it_ref[s]
    j = jt_ref[s]

    @pl.when(j == 0)
    def _():
        m_sc[...] = jnp.full(m_sc.shape, M_INIT, F32)
        l_sc[...] = jnp.zeros_like(l_sc)
        acc_sc[...] = jnp.zeros_like(acc_sc)

    def logits(near):
        if mode == "dsa":
            sel_bias = jnp.concatenate([b1_ref[u] for u in range(t // LANES)], axis=1).astype(F32)
        for h in range(n_heads):
            hs = slice(h * hd, (h + 1) * hd)
            st = jnp.dot(k_ref[:, hs], qt_ref[0, hs, :], preferred_element_type=F32) * (scale * LOG2E)
            if mode == "fox":
                st = st - jnp.concatenate([b1_ref[h]] * (t // LANES), axis=1)
                if near:
                    st = st + b2_ref[...]
            elif near:
                st = st + b2_ref[h, i - j] + sel_bias
            else:
                st = st + far_ref[h] + sel_bias
            s_sc[h] = st
            mx_sc[h] = jnp.max(st, axis=0, keepdims=True)

    n_near = 1 if mode == "fox" else 2
    pl.when(i - j < n_near)(functools.partial(logits, True))
    pl.when(i - j >= n_near)(functools.partial(logits, False))

    for h in range(n_heads):
        hs = slice(h * hd, (h + 1) * hd)
        st = s_sc[h]
        m_old = m_sc[h]
        m_new = jnp.maximum(m_old, mx_sc[h])
        alpha = jnp.exp2(m_old - m_new)
        p = jnp.exp2(st - m_new)
        l_sc[h] = alpha * l_sc[h] + jnp.sum(p, axis=0, keepdims=True)
        pv = jnp.dot(vt_ref[0, hs, :], p.astype(BF16), preferred_element_type=F32)
        acc_sc[hs, :] = alpha * acc_sc[hs, :] + pv
        m_sc[h] = m_new

    @pl.when(j == i)
    def _():
        for h in range(n_heads):
            hs = slice(h * hd, (h + 1) * hd)
            o_ref[0, hs, :] = (acc_sc[hs, :] / l_sc[h]).astype(BF16)


def _attn_prompt(mode, qt, k, vt, b1, b2, far, *, t, n_heads, hd):
    nb, width, _ = qt.shape
    s = nb * t
    assert t % LANES == 0
    pairs = [(i, j) for i in range(nb) for j in range(i + 1)]
    it = jnp.asarray([p[0] for p in pairs], I32)
    jt = jnp.asarray([p[1] for p in pairs], I32)
    if mode == "fox":
        b1_spec = pl.BlockSpec((n_heads, t, LANES), lambda g, it, jt: (0, jt[g], 0))
    else:
        b1_spec = pl.BlockSpec((t // LANES, t, LANES), lambda g, it, jt: (it[g], jt[g], 0))
    grid_spec = pltpu.PrefetchScalarGridSpec(
        num_scalar_prefetch=2,
        grid=(len(pairs),),
        in_specs=[
            pl.BlockSpec((1, width, t), lambda g, it, jt: (it[g], 0, 0)),
            pl.BlockSpec((t, width), lambda g, it, jt: (jt[g], 0)),
            pl.BlockSpec((1, width, t), lambda g, it, jt: (jt[g], 0, 0)),
            b1_spec,
            pl.BlockSpec(memory_space=pltpu.VMEM),
            pl.BlockSpec(memory_space=pltpu.SMEM),
        ],
        out_specs=pl.BlockSpec((1, width, t), lambda g, it, jt: (it[g], 0, 0)),
        scratch_shapes=[
            pltpu.VMEM((n_heads, 1, t), F32),
            pltpu.VMEM((n_heads, 1, t), F32),
            pltpu.VMEM((width, t), F32),
            pltpu.VMEM((n_heads, t, t), F32),
            pltpu.VMEM((n_heads, 1, t), F32),
        ],
    )
    return pl.pallas_call(
        functools.partial(_attn_prompt_kernel, mode=mode, t=t, n_heads=n_heads, hd=hd,
                          scale=hd ** -0.5),
        grid_spec=grid_spec,
        out_shape=jax.ShapeDtypeStruct((nb, width, t), BF16),
        compiler_params=_cparams(("arbitrary",)),
        name="attn_prompt_" + mode,
    )(it, jt, qt, k, vt, b1, b2, far)


def _idx_sample_kernel(pt_ref, *refs, n_sel, pps, page, n_steps, t_new, n_heads, group):
    page_refs = refs[:pps]
    qi_ref, w_ref, kn_ref, mask_ref, sc_sc = refs[pps:]
    st = pl.program_id(1)
    span = pps * page
    width = sc_sc.shape[1]
    n_past = n_steps * span
    qi = qi_ref[0]
    w = w_ref[0]

    def scores(kb):
        sc = lax.dot_general(qi, kb, (((1,), (1,)), ((), ())), preferred_element_type=F32)
        sc = jnp.maximum(sc, 0.0) * jnp.concatenate([w] * (kb.shape[0] // LANES), axis=1)
        acc = sc[0:SUBLANES, :]
        for h in range(1, n_heads):
            acc = acc + sc[h * SUBLANES:(h + 1) * SUBLANES, :]
        return acc

    slot = pl.program_id(0) % group
    r0 = pl.multiple_of(slot * SUBLANES, SUBLANES)
    kb = jnp.concatenate([r[0].astype(BF16) for r in page_refs], axis=0)
    c0 = pl.multiple_of(st * span, span)
    sc_sc[pl.ds(r0, SUBLANES), pl.ds(c0, span)] = _sort_key(scores(kb))

    @pl.when(st == n_steps - 1)
    def _():
        qrow = lax.broadcasted_iota(I32, (SUBLANES, page), 0)
        jcol = lax.broadcasted_iota(I32, (SUBLANES, page), 1)
        new = jnp.where((jcol <= qrow) & (jcol < t_new), scores(kn_ref[0]), NEG_INF)
        sc_sc[pl.ds(r0, SUBLANES), n_past:n_past + page] = _sort_key(new)

    @pl.when((st == n_steps - 1) & (slot == group - 1))
    def _():
        rows = group * SUBLANES

        def bisect(b, ans_u):
            cand_u = ans_u | jnp.left_shift(jnp.int32(1), 31 - b)
            cand = cand_u ^ jnp.int32(INT_MIN)
            cnt = jnp.sum(jnp.where(sc_sc[...] >= cand, 1, 0).astype(I32), axis=1, keepdims=True)
            return jnp.where(cnt >= n_sel, cand_u, ans_u)

        ans_u = lax.fori_loop(0, 32, bisect, jnp.zeros((rows, 1), I32))
        thr = ans_u ^ jnp.int32(INT_MIN)
        n_gt = jnp.sum(jnp.where(sc_sc[...] > thr, 1, 0).astype(I32), axis=1, keepdims=True)
        need = (n_sel - n_gt).astype(F32)

        r = lax.broadcasted_iota(I32, (LANES, LANES), 0)
        c = lax.broadcasted_iota(I32, (LANES, LANES), 1)
        strict_upper = jnp.where(r < c, 1.0, 0.0).astype(BF16)
        seen = jnp.zeros((rows, 1), F32)
        for blk in range(width // LANES):
            cols = slice(blk * LANES, (blk + 1) * LANES)
            kblk = sc_sc[:, cols]
            eq = jnp.where(kblk == thr, 1.0, 0.0)
            before = jnp.dot(eq.astype(BF16), strict_upper, preferred_element_type=F32) + seen
            tie_ok = jnp.where(before < need, eq, 0.0)
            sel = jnp.where(kblk > thr, 1.0, tie_ok)
            mask_ref[:, :, cols] = jnp.where(sel > 0.0, 0.0, NEG_INF).reshape(group, SUBLANES, LANES)
            seen = seen + jnp.sum(eq, axis=1, keepdims=True)


def _idx_sample(pt_flat, pool_ik, qi, w, kn, *, n_sel, n_pages, t_new, n_heads, pps=16, group=8):
    b = qi.shape[0]
    page, idim = pool_ik.shape[1:]
    pps = math.gcd(pps, n_pages)
    group = math.gcd(group, b)
    assert page == LANES
    n_steps = n_pages // pps
    width = n_pages * page + page

    def page_spec(r):
        return pl.BlockSpec((1, page, idim),
                            lambda bi, st, pt: (pt[bi * n_pages + st * pps + r], 0, 0))

    grid_spec = pltpu.PrefetchScalarGridSpec(
        num_scalar_prefetch=1,
        grid=(b, n_steps),
        in_specs=[page_spec(r) for r in range(pps)] + [
            pl.BlockSpec((1, n_heads * SUBLANES, idim), lambda bi, st, pt: (bi, 0, 0)),
            pl.BlockSpec((1, n_heads * SUBLANES, LANES), lambda bi, st, pt: (bi, 0, 0)),
            pl.BlockSpec((1, page, idim), lambda bi, st, pt: (bi, 0, 0)),
        ],
        out_specs=pl.BlockSpec((group, SUBLANES, width), lambda bi, st, pt: (bi // group, 0, 0)),
        scratch_shapes=[pltpu.VMEM((group * SUBLANES, width), I32)],
    )
    return pl.pallas_call(
        functools.partial(_idx_sample_kernel, n_sel=n_sel, pps=pps, page=page, n_steps=n_steps,
                          t_new=t_new, n_heads=n_heads, group=group),
        grid_spec=grid_spec,
        out_shape=jax.ShapeDtypeStruct((b, SUBLANES, width), F32),
        compiler_params=_cparams(("arbitrary", "arbitrary")),
        name="idx_sample",
    )(pt_flat, *([pool_ik] * pps), qi, w, kn)


def _attn_sample_kernel(pt_ref, *refs, mode, pps, page, n_steps, t_new, n_heads, hd, scale):
    k_refs = refs[:pps]
    v_refs = refs[pps:2 * pps]
    rest = refs[2 * pps:]
    if mode == "fox":
        lf_refs = rest[:pps]
        q_ref, kn_ref, vn_ref, lfn_ref, o_ref, m_sc, l_sc, acc_sc, carry_sc = rest[pps:]
    else:
        q_ref, kn_ref, vn_ref, mask_ref, t5_ref, o_ref, m_sc, l_sc, acc_sc = rest
    st = pl.program_id(1)
    rows = SUBLANES * n_heads
    n_past = n_steps * pps * page

    @pl.when(st == 0)
    def _():
        m_sc[...] = jnp.full(m_sc.shape, M_INIT, F32)
        l_sc[...] = jnp.zeros_like(l_sc)
        acc_sc[...] = jnp.zeros_like(acc_sc)
        if mode == "fox":
            carry_sc[...] = jnp.zeros_like(carry_sc)

    span = pps * page
    is_last = st == n_steps - 1

    if mode == "fox":
        ru = lax.broadcasted_iota(I32, (page, page), 0)
        cu = lax.broadcasted_iota(I32, (page, page), 1)
        lower_incl = jnp.where(cu <= ru, 1.0, 0.0).astype(BF16)
        blocks = []
        before = jnp.zeros((1, 3 * n_heads), F32)
        for r in list(lf_refs) + [lfn_ref]:
            c = jnp.dot(lower_incl, jnp.concatenate(_split3(r[0]), axis=1), preferred_element_type=F32)
            blocks.append(c + before)
            before = before + c[page - 1:page, :]
        cum_parts = jnp.concatenate(_split3(jnp.concatenate(blocks, axis=0)), axis=1)
        rh = lax.broadcasted_iota(I32, (rows, 9 * n_heads), 0) // SUBLANES
        ch = lax.broadcasted_iota(I32, (rows, 9 * n_heads), 1) % n_heads
        head_of_row = jnp.where(rh == ch, 1.0, 0.0).astype(BF16)
        cum_t = lax.dot_general(head_of_row, cum_parts, (((1,), (1,)), ((), ())),
                                preferred_element_type=F32)
        f_cum = cum_t + carry_sc[...]
        carry_sc[...] = f_cum[:, span - 1:span]
        qrow = lax.broadcasted_iota(I32, (rows, page), 0) % SUBLANES
        jcol = lax.broadcasted_iota(I32, (rows, page), 1)
        new_ok = (jcol <= qrow) & (jcol < t_new) & is_last
        bias = jnp.concatenate([-f_cum[:, :span], jnp.where(new_ok, -f_cum[:, span:], NEG_INF)], axis=1)
    else:
        def mask_rows(mk):
            return jnp.concatenate([mk] * n_heads, axis=0)

        c0 = pl.multiple_of(st * span, span)
        past_bias = t5_ref[:, pl.ds(c0, span)] + mask_rows(mask_ref[0, :, pl.ds(c0, span)])
        new_bias = t5_ref[:, n_past:n_past + page] + mask_rows(mask_ref[0, :, n_past:n_past + page])
        bias = jnp.concatenate([past_bias, jnp.where(is_last, new_bias, NEG_INF)], axis=1)

    def head_rows(page_refs, new_ref, h):
        return jnp.concatenate(
            [r[0, pl.ds(h, page, stride=n_heads), :].astype(BF16) for r in page_refs] + [new_ref[0, h]], axis=0)

    s = jnp.concatenate(
        [lax.dot_general(q_ref[0, h * SUBLANES:(h + 1) * SUBLANES, :], head_rows(k_refs, kn_ref, h),
                         (((1,), (1,)), ((), ())), preferred_element_type=F32) for h in range(n_heads)], axis=0)
    s = s * scale + bias
    m_old = m_sc[...]
    m_new = jnp.maximum(m_old, jnp.max(s, axis=1, keepdims=True))
    alpha = jnp.exp(m_old - m_new)
    p = jnp.exp(s - m_new)
    l_sc[...] = alpha * l_sc[...] + jnp.sum(p, axis=1, keepdims=True)
    p16 = p.astype(BF16)
    pv = jnp.concatenate(
        [jnp.dot(p16[h * SUBLANES:(h + 1) * SUBLANES, :], head_rows(v_refs, vn_ref, h),
                 preferred_element_type=F32) for h in range(n_heads)], axis=0)
    acc_sc[...] = alpha * acc_sc[...] + pv
    m_sc[...] = m_new

    @pl.when(is_last)
    def _():
        o_ref[0] = (acc_sc[...] / l_sc[...]).astype(BF16)


def _attn_sample(mode, pt_flat, pool_k, pool_v, q, kn, vn, extra, *, n_pages, t_new, n_heads, hd, pps=16):
    b, rows, _ = q.shape
    page = pool_k.shape[1] // n_heads
    pps = math.gcd(pps, n_pages)
    assert page == LANES and rows == SUBLANES * n_heads and t_new <= SUBLANES
    n_steps = n_pages // pps

    def page_spec(shape_tail, r):
        zeros = (0,) * len(shape_tail)
        return pl.BlockSpec((1,) + shape_tail,
                            lambda bi, st, pt: (pt[bi * n_pages + st * pps + r],) + zeros)

    def per_req(shape_tail):
        zeros = (0,) * len(shape_tail)
        return pl.BlockSpec((1,) + shape_tail, lambda bi, st, pt: (bi,) + zeros)

    in_specs = [page_spec((page * n_heads, hd), r) for r in range(pps)] * 2
    args = [pool_k] * pps + [pool_v] * pps
    scratch = [pltpu.VMEM((rows, 1), F32), pltpu.VMEM((rows, 1), F32), pltpu.VMEM((rows, hd), F32)]
    shared = [per_req((rows, hd)), per_req((n_heads, page, hd)), per_req((n_heads, page, hd))]
    if mode == "fox":
        pool_lf, lfn = extra
        in_specs += [page_spec((page, n_heads), r) for r in range(pps)]
        args += [pool_lf] * pps
        in_specs += shared + [per_req((page, n_heads))]
        args += [q, kn, vn, lfn]
        scratch.append(pltpu.VMEM((rows, 1), F32))
    else:
        mask, t5 = extra
        in_specs += shared + [per_req(mask.shape[1:]), pl.BlockSpec(memory_space=pltpu.VMEM)]
        args += [q, kn, vn, mask, t5]
    grid_spec = pltpu.PrefetchScalarGridSpec(
        num_scalar_prefetch=1,
        grid=(b, n_steps),
        in_specs=in_specs,
        out_specs=pl.BlockSpec((1, rows, hd), lambda bi, st, pt: (bi, 0, 0)),
        scratch_shapes=scratch,
    )
    return pl.pallas_call(
        functools.partial(_attn_sample_kernel, mode=mode, pps=pps, page=page, n_steps=n_steps,
                          t_new=t_new, n_heads=n_heads, hd=hd, scale=hd ** -0.5),
        grid_spec=grid_spec,
        out_shape=jax.ShapeDtypeStruct((b, rows, hd), BF16),
        compiler_params=_cparams(("arbitrary", "arbitrary")),
        name="attn_sample_" + mode,
    )(pt_flat, *args)


def _t5_bucket(dist, num_buckets):
    n = jnp.maximum(dist, 0)
    exact = num_buckets // 2
    nf = jnp.maximum(n, 1).astype(F32)
    large = exact + (jnp.log(nf / exact) / math.log(MAX_DISTANCE / exact)
                     * (num_buckets - exact)).astype(I32)
    return jnp.where(n < exact, n, jnp.minimum(large, num_buckets - 1))


def _t5_bias(rel_bias, dist):
    nb = rel_bias.shape[0]
    onehot = (_t5_bucket(dist, nb)[..., None] == jnp.arange(nb)).astype(F32)
    b = jnp.einsum("...n,nh->h...", onehot, rel_bias.astype(F32), precision=lax.Precision.HIGHEST)
    return jnp.where(dist[None] >= 0, b, NEG_INF)


def _is_pow2(x):
    return math.frexp(x)[0] == 0.5


def _mix_prompt(q_a, k_a, v_a, lf_a, q_b, k_b, v_b, q_i, k_i, w_i, rel_bias, *, n_heads, hd, t=512):
    s = q_a.shape[0]
    width = n_heads * hd
    to_blocks = lambda a: a.reshape(s // t, t, width).transpose(0, 2, 1)
    from_blocks = lambda a: a.transpose(0, 2, 1).reshape(s, width)
    idim = k_i.shape[1]
    ih = w_i.shape[1]
    n_sel = min(TOPK_MAX, s // 4)
    assert MAX_DISTANCE <= t + 1
    idx_scale, idx_w_scale = idim ** -0.5, ih ** -0.5
    assert _is_pow2(idx_scale) and _is_pow2(idx_w_scale)

    fk = _cumsum_rep(lf_a, out_scale=LOG2E)
    r = jnp.arange(t)
    dist0 = r[None, :] - r[:, None]
    causal = jnp.where(dist0 >= 0, 0.0, NEG_INF).astype(F32)
    o_a = from_blocks(_attn_prompt("fox", to_blocks(q_a), k_a, to_blocks(v_a), fk, causal,
                                   jnp.zeros((n_heads,), F32), t=t, n_heads=n_heads, hd=hd))

    tq = LANES
    nb = s // tq
    qt = q_i.reshape(nb, tq, ih, idim).transpose(0, 3, 2, 1).reshape(nb, idim, ih * tq)
    w = (w_i * (idx_scale * idx_w_scale)).reshape(nb, tq, ih).transpose(0, 2, 1)
    mask = _idx_prompt(k_i, qt, w, n_sel=n_sel, tq=tq)
    btiles = jnp.stack([_t5_bias(rel_bias, dist0 + nd * t) for nd in range(2)], axis=1) * LOG2E
    far = _t5_bias(rel_bias, jnp.full((1,), 2 * t, I32))[:, 0] * LOG2E
    o_b = from_blocks(_attn_prompt("dsa", to_blocks(q_b), k_b, to_blocks(v_b), mask, btiles, far,
                                   t=t, n_heads=n_heads, hd=hd))
    return o_a, o_b


def _head_major(x, n_heads, hd, t_pad):
    b, t_new, _ = x.shape
    xh = x.reshape(b, t_new, n_heads, hd).transpose(0, 2, 1, 3)
    return jnp.pad(xh, ((0, 0), (0, 0), (0, t_pad - t_new), (0, 0)))


def _pad_rows(x, rows):
    return jnp.pad(x, ((0, 0), (0, rows - x.shape[1]), (0, 0)))


def _mix_sample(q_a, k_a, v_a, lf_a, q_b, k_b, v_b, q_i, k_i, w_i, rel_bias, page_table,
                fox_k, fox_v, fox_lf, dsa_k, dsa_v, idx_k, *, layer, n_heads, hd):
    b, t_new, width = q_a.shape
    n_pages = page_table.shape[1]
    depth, n_pool, page = fox_k.shape[:3]
    idim = k_i.shape[2]
    ih = w_i.shape[2]
    past = n_pages * page
    n_sel = min(TOPK_MAX, (past + t_new) // 4)
    idx_scale, idx_w_scale = idim ** -0.5, ih ** -0.5
    assert _is_pow2(idx_scale) and _is_pow2(idx_w_scale)
    assert t_new <= SUBLANES
    pt_flat = page_table.reshape(-1).astype(I32) + layer * n_pool
    n_pool = depth * n_pool
    fox_lf = fox_lf.reshape(n_pool, page, n_heads)
    idx_k = idx_k.reshape(n_pool, page, idim)

    flat = lambda pool: pool.reshape(n_pool, page * n_heads, hd)
    q_rows = lambda q: _head_major(q, n_heads, hd, SUBLANES).reshape(b, n_heads * SUBLANES, hd)
    new_keys = lambda x: _head_major(x, n_heads, hd, page)
    o_a = _attn_sample("fox", pt_flat, flat(fox_k), flat(fox_v), q_rows(q_a), new_keys(k_a), new_keys(v_a),
                       (fox_lf, _pad_rows(lf_a, page)), n_pages=n_pages, t_new=t_new, n_heads=n_heads, hd=hd)

    qi = jnp.pad(q_i.reshape(b, t_new, ih, idim), ((0, 0), (0, SUBLANES - t_new), (0, 0), (0, 0)))
    qi = qi.transpose(0, 2, 1, 3).reshape(b, ih * SUBLANES, idim)
    wr = jnp.pad(w_i * (idx_scale * idx_w_scale), ((0, 0), (0, SUBLANES - t_new), (0, 0)))
    wr = jnp.broadcast_to(wr.transpose(0, 2, 1).reshape(b, ih * SUBLANES, 1), (b, ih * SUBLANES, LANES))
    mask = _idx_sample(pt_flat, idx_k, qi, wr, _pad_rows(k_i, page), n_sel=n_sel, n_pages=n_pages,
                       t_new=t_new, n_heads=ih)
    kpos = jnp.arange(past + page)
    qpos = past + jnp.arange(t_new)
    dist = jnp.where(kpos[None, :] < past + t_new, qpos[:, None] - kpos[None, :], -1)
    t5 = _t5_bias(rel_bias, dist)
    t5 = jnp.pad(t5, ((0, 0), (0, SUBLANES - t_new), (0, 0))).reshape(n_heads * SUBLANES, past + page)
    o_b = _attn_sample("dsa", pt_flat, flat(dsa_k), flat(dsa_v), q_rows(q_b), new_keys(k_b), new_keys(v_b),
                       (mask, t5), n_pages=n_pages, t_new=t_new, n_heads=n_heads, hd=hd)
    tokens = lambda o: o.reshape(b, n_heads, SUBLANES, hd)[:, :, :t_new].transpose(0, 2, 1, 3).reshape(b * t_new, width)
    return tokens(o_a), tokens(o_b)


def kernel(x_prompt, x_sample, cache_fox_k, cache_fox_v, cache_fox_logf, cache_dsa_k, cache_dsa_v, cache_idx_k, page_table, p_prompt, p_sample, rel_bias, g_ffn1, w1_pre, w3_pre, w2_pre, g_mix, w_in, b_forget, w_branch_fox, w_branch_dsa, w_out, g_ffn2, w1_post, w3_post, w2_post, g_ple, w_ple, w_ple_gate, g_final):
    depth = w_in.shape[0]
    bp, sp, d = x_prompt.shape
    bs, ts, _ = x_sample.shape
    assert bp == 1
    n_fox, hd = cache_fox_k.shape[3:]
    n_dsa = cache_dsa_k.shape[3]
    assert n_fox == n_dsa
    idim = cache_idx_k.shape[-1]
    ih = IDX_HEADS
    fw, dw = n_fox * hd, n_dsa * hd
    n_in = w_in.shape[2]
    assert n_in == 3 * fw + n_fox + 3 * dw + ih * idim + idim + ih + 2 * d
    mp, ms = bp * sp, bs * ts
    small_w = n_fox + idim + ih
    assert small_w <= LANES

    x = jnp.concatenate([x_prompt.reshape(mp, d), x_sample.reshape(ms, d)], axis=0)
    st_p, st_s = [], []
    for l in range(depth):
        offs = np.cumsum([0, fw, fw, fw, n_fox, dw, dw, dw, ih * idim, idim, ih, 2 * d])
        col = lambda a, c: w_in[l][:, offs[a]:offs[a] + c]
        w_q = jnp.concatenate([col(0, fw), col(4, dw), col(7, ih * idim)], axis=1).astype(BF16)
        w_kv = [col(c, fw).astype(BF16) for c in (1, 2, 5, 6)]
        w_gate = col(10, 2 * d).astype(BF16)
        w_small = jnp.concatenate([col(3, n_fox), col(8, idim), col(9, ih)], axis=1)
        w_small = jnp.pad(w_small, ((0, 0), (0, LANES - small_w))).astype(BF16)
        b_small = jnp.pad(b_forget[l], (0, LANES - n_fox)).reshape(1, LANES)

        x1, h_mix = _ffn(x, g_ffn1[l], w1_pre[l].astype(BF16), w3_pre[l].astype(BF16),
                         w2_pre[l].astype(BF16), g_mix[l])
        zq = _proj("q", h_mix, w_q)
        kv_p, kv_s, kv16 = zip(*[_proj_kv(h_mix, w, m_prompt=mp, n_heads=n_fox, hd=hd) for w in w_kv])
        gate = _proj("gate", h_mix, w_gate)
        small = _proj("small", h_mix, w_small, bias=b_small, n_fox=n_fox)

        q_a, q_b, q_i = zq[:, :fw], zq[:, fw:fw + dw], zq[:, fw + dw:]
        lf = small[:, :n_fox]
        k_i32 = small[:, n_fox:n_fox + idim]
        w_i = small[:, n_fox + idim:small_w]
        k_i16 = k_i32.astype(BF16)

        o_ap, o_bp = _mix_prompt(q_a[:mp], kv16[0], kv16[1][:mp], lf[:mp], q_b[:mp], kv16[2],
                                 kv16[3][:mp], q_i[:mp], k_i16[:mp], w_i[:mp], rel_bias,
                                 n_heads=n_fox, hd=hd)
        rs = lambda a: a[mp:].reshape(bs, ts, -1)
        o_as, o_bs = _mix_sample(rs(q_a), rs(kv16[0]), rs(kv16[1]), rs(lf), rs(q_b), rs(kv16[2]),
                                 rs(kv16[3]), rs(q_i), rs(k_i16), rs(w_i), rel_bias, page_table,
                                 cache_fox_k, cache_fox_v, cache_fox_logf, cache_dsa_k,
                                 cache_dsa_v, cache_idx_k, layer=l, n_heads=n_fox, hd=hd)
        o_a = jnp.concatenate([o_ap, o_as], axis=0)
        o_b = jnp.concatenate([o_bp, o_bs], axis=0)

        merged = _merge(o_a, o_b, gate, w_branch_fox[l].astype(BF16), w_branch_dsa[l].astype(BF16))
        x2 = _resid_mm(x1, merged, w_out[l].astype(BF16))
        x3, h_ple = _ffn(x2, g_ffn2[l], w1_post[l].astype(BF16), w3_post[l].astype(BF16),
                         w2_post[l].astype(BF16), g_ple[l])
        p = jnp.concatenate([p_prompt[l].reshape(mp, -1), p_sample[l].reshape(ms, -1)], axis=0).astype(BF16)
        ple_w = (w_ple[l].astype(BF16), w_ple_gate[l].astype(BF16), g_final)
        if l + 1 < depth:
            x, = _ple(x3, p, h_ple, *ple_w)
        else:
            y_p, y_s = _ple(x3, p, h_ple, *ple_w, m_prompt=mp)

        heads = lambda a, bb, tt: a.reshape(bb, tt, n_fox, hd)
        st_p.append((heads(kv_p[0], bp, sp), heads(kv_p[1], bp, sp), lf[:mp].reshape(bp, sp, n_fox),
                     heads(kv_p[2], bp, sp), heads(kv_p[3], bp, sp), k_i32[:mp].reshape(bp, sp, idim)))
        st_s.append((heads(kv_s[0], bs, ts), heads(kv_s[1], bs, ts), lf[mp:].reshape(bs, ts, n_fox),
                     heads(kv_s[2], bs, ts), heads(kv_s[3], bs, ts), k_i32[mp:].reshape(bs, ts, idim)))

    pfk, pfv, pflf, pdk, pdv, pik = [jnp.stack(a) for a in zip(*st_p)]
    sfk, sfv, sflf, sdk, sdv, sik = [jnp.stack(a) for a in zip(*st_s)]
    y_prompt = y_p.reshape(bp, sp, d)
    y_sample = y_s.reshape(bs, ts, d)
    return (y_prompt, y_sample, pfk, pfv, pflf, pdk, pdv, pik, sfk, sfv, sflf, sdk, sdv, sik)
```

```python
import functools
import math

import numpy as np
import jax
import jax.numpy as jnp
from jax import lax
from jax.experimental import pallas as pl
from jax.experimental.pallas import tpu as pltpu

F32 = jnp.float32
BF16 = jnp.bfloat16
I32 = jnp.int32
I16 = jnp.int16
HALF_BIAS = 1 << 15

TOPK_MAX = 256
MAX_DISTANCE = 128
RMS_EPS = 1e-6
IDX_HEADS = 16

V7X_VMEM_BYTES = 64 * 1024 * 1024
VMEM_LIMIT = V7X_VMEM_BYTES - 8 * 1024 * 1024
LANES = 128
SUBLANES = 8

LOG2E = math.log2(math.e)
NEG_INF = float("-inf")
M_INIT = -1e30
INT_MIN = -2 ** 31
NEG_INF_KEY = int(np.int32(np.uint32(0xFF800000) ^ np.uint32(0x7FFFFFFF)))


def _cparams(sem):
    return pltpu.CompilerParams(dimension_semantics=sem, vmem_limit_bytes=VMEM_LIMIT)


def _rms(x, g):
    ms = jnp.mean(x * x, axis=-1, keepdims=True)
    return x * lax.rsqrt(ms + RMS_EPS) * g


def _sigmoid(x):
    return 1.0 / (1.0 + jnp.exp(-x))


def _key_hi(key):
    return key >> 16


def _key_lo(key):
    return (key & 0xFFFF) - HALF_BIAS


def _split3(x):
    hi = x.astype(BF16)
    r1 = x - hi.astype(F32)
    mid = r1.astype(BF16)
    lo = (r1 - mid.astype(F32)).astype(BF16)
    return hi, mid, lo


def _sort_key(x):
    bits = pltpu.bitcast(x, I32)
    return bits ^ ((bits >> 31) & jnp.int32(0x7FFFFFFF))


def _ffn_kernel(x_ref, g_ref, w1_ref, w3_ref, w2_ref, gn_ref, o_ref, hn_ref, h_sc, acc_sc):
    f = pl.program_id(1)

    @pl.when(f == 0)
    def _():
        h_sc[...] = _rms(x_ref[...], g_ref[...]).astype(BF16)
        acc_sc[...] = jnp.zeros_like(acc_sc)

    h = h_sc[...]
    a = jnp.dot(h, w1_ref[...], preferred_element_type=F32)
    b = jnp.dot(h, w3_ref[...], preferred_element_type=F32)
    act = (a * _sigmoid(a) * b).astype(BF16)
    acc_sc[...] += jnp.dot(act, w2_ref[...], preferred_element_type=F32)

    @pl.when(f == pl.num_programs(1) - 1)
    def _():
        x1 = x_ref[...] + 0.5 * acc_sc[...]
        o_ref[...] = x1
        hn_ref[...] = _rms(x1, gn_ref[...]).astype(BF16)


def _ffn(x, g, w1, w3, w2, g_next, *, tm=512, tf=512):
    m, d = x.shape
    dff = w1.shape[1]
    assert m % tm == 0 and dff % tf == 0
    return pl.pallas_call(
        _ffn_kernel,
        grid=(m // tm, dff // tf),
        in_specs=[
            pl.BlockSpec((tm, d), lambda i, f: (i, 0)),
            pl.BlockSpec((1, d), lambda i, f: (0, 0)),
            pl.BlockSpec((d, tf), lambda i, f: (0, f)),
            pl.BlockSpec((d, tf), lambda i, f: (0, f)),
            pl.BlockSpec((tf, d), lambda i, f: (f, 0)),
            pl.BlockSpec((1, d), lambda i, f: (0, 0)),
        ],
        out_specs=[
            pl.BlockSpec((tm, d), lambda i, f: (i, 0)),
            pl.BlockSpec((tm, d), lambda i, f: (i, 0)),
        ],
        out_shape=[jax.ShapeDtypeStruct((m, d), F32), jax.ShapeDtypeStruct((m, d), BF16)],
        scratch_shapes=[pltpu.VMEM((tm, d), BF16), pltpu.VMEM((tm, d), F32)],
        compiler_params=_cparams(("parallel", "arbitrary")),
        name="ffn",
    )(x, g.reshape(1, d), w1, w3, w2, g_next.reshape(1, d))


def _proj_q_kernel(h_ref, w_ref, o16_ref):
    o16_ref[...] = jnp.dot(h_ref[...], w_ref[...], preferred_element_type=F32).astype(BF16)


def _proj_kv_kernel(h_ref, w_ref, op_ref, os_ref, o16_ref, *, n_heads, hd, prompt_tiles):
    i = pl.program_id(0)
    z = jnp.dot(h_ref[...], w_ref[...], preferred_element_type=F32)
    o16_ref[...] = z.astype(BF16)

    def put(dst):
        for h in range(n_heads):
            dst[:, h, :] = z[:, h * hd:(h + 1) * hd]

    pl.when(i < prompt_tiles)(functools.partial(put, op_ref))
    pl.when(i >= prompt_tiles)(functools.partial(put, os_ref))


def _proj_kv(h, w, *, m_prompt, n_heads, hd, tm=512):
    m, k = h.shape
    n = w.shape[1]
    m_sample = m - m_prompt
    assert n == n_heads * hd and m_prompt % tm == 0 and m_sample % tm == 0
    pt = m_prompt // tm
    return pl.pallas_call(
        functools.partial(_proj_kv_kernel, n_heads=n_heads, hd=hd, prompt_tiles=pt),
        grid=(m // tm,),
        in_specs=[pl.BlockSpec((tm, k), lambda i: (i, 0)), pl.BlockSpec((k, n), lambda i: (0, 0))],
        out_specs=[
            pl.BlockSpec((tm, n_heads, hd), lambda i: (jnp.minimum(i, pt - 1), 0, 0)),
            pl.BlockSpec((tm, n_heads, hd), lambda i: (jnp.maximum(i - pt, 0), 0, 0)),
            pl.BlockSpec((tm, n), lambda i: (i, 0)),
        ],
        out_shape=[jax.ShapeDtypeStruct((m_prompt, n_heads, hd), F32),
                   jax.ShapeDtypeStruct((m_sample, n_heads, hd), F32),
                   jax.ShapeDtypeStruct((m, n), BF16)],
        compiler_params=_cparams(("arbitrary",)),
        name="proj_kv",
    )(h, w)


def _proj_gate_kernel(h_ref, w_ref, o_ref):
    z = jnp.dot(h_ref[...], w_ref[...], preferred_element_type=F32)
    o_ref[...] = _sigmoid(z)


def _proj_small_kernel(h_ref, w_ref, b_ref, o_ref, *, n_fox):
    z = jnp.dot(h_ref[...], w_ref[...], preferred_element_type=F32)
    lane = lax.broadcasted_iota(I32, z.shape, 1)
    zf = z + b_ref[...]
    logsig = jnp.minimum(zf, 0.0) - jnp.log1p(jnp.exp(-jnp.abs(zf)))
    o_ref[...] = jnp.where(lane < n_fox, logsig, z)


def _proj(kind, h, w, *, tm=512, tn=1024, bias=None, n_fox=0):
    m, k = h.shape
    n = w.shape[1]
    tn = min(tn, n)
    assert m % tm == 0 and n % tn == 0
    grid = (n // tn, m // tm)
    h_spec = pl.BlockSpec((tm, k), lambda j, i: (i, 0))
    w_spec = pl.BlockSpec((k, tn), lambda j, i: (0, j))
    o_spec = pl.BlockSpec((tm, tn), lambda j, i: (i, j))
    cp = _cparams(("parallel", "parallel"))
    if kind == "q":
        return pl.pallas_call(
            _proj_q_kernel, grid=grid, in_specs=[h_spec, w_spec], out_specs=o_spec,
            out_shape=jax.ShapeDtypeStruct((m, n), BF16), compiler_params=cp, name="proj_q")(h, w)
    if kind == "gate":
        return pl.pallas_call(
            _proj_gate_kernel, grid=grid, in_specs=[h_spec, w_spec], out_specs=o_spec,
            out_shape=jax.ShapeDtypeStruct((m, n), F32), compiler_params=cp, name="proj_gate")(h, w)
    assert kind == "small"
    b_spec = pl.BlockSpec((1, tn), lambda j, i: (0, j))
    return pl.pallas_call(
        functools.partial(_proj_small_kernel, n_fox=n_fox), grid=grid,
        in_specs=[h_spec, w_spec, b_spec], out_specs=o_spec,
        out_shape=jax.ShapeDtypeStruct((m, n), F32), compiler_params=cp, name="proj_small")(h, w, bias)


def _merge_kernel(oa_ref, ob_ref, g_ref, wa_ref, wb_ref, m_ref, *, d):
    ua = jnp.dot(oa_ref[...], wa_ref[...], preferred_element_type=F32)
    ub = jnp.dot(ob_ref[...], wb_ref[...], preferred_element_type=F32)
    m_ref[...] = (g_ref[:, :d] * ua + g_ref[:, d:] * ub).astype(BF16)


def _merge(oa, ob, gate, wa, wb, *, tm=256):
    m, wdt = oa.shape
    d = wa.shape[1]
    assert m % tm == 0
    return pl.pallas_call(
        functools.partial(_merge_kernel, d=d),
        grid=(m // tm,),
        in_specs=[
            pl.BlockSpec((tm, wdt), lambda i: (i, 0)),
            pl.BlockSpec((tm, wdt), lambda i: (i, 0)),
            pl.BlockSpec((tm, 2 * d), lambda i: (i, 0)),
            pl.BlockSpec((wdt, d), lambda i: (0, 0)),
            pl.BlockSpec((wdt, d), lambda i: (0, 0)),
        ],
        out_specs=pl.BlockSpec((tm, d), lambda i: (i, 0)),
        out_shape=jax.ShapeDtypeStruct((m, d), BF16),
        compiler_params=_cparams(("parallel",)),
        name="merge",
    )(oa, ob, gate, wa, wb)


def _resid_mm_kernel(x_ref, m_ref, w_ref, o_ref):
    o_ref[...] = x_ref[...] + jnp.dot(m_ref[...], w_ref[...], preferred_element_type=F32)


def _resid_mm(x, mm, w, *, tm=256):
    m, d = x.shape
    k = mm.shape[1]
    assert m % tm == 0
    return pl.pallas_call(
        _resid_mm_kernel,
        grid=(m // tm,),
        in_specs=[
            pl.BlockSpec((tm, d), lambda i: (i, 0)),
            pl.BlockSpec((tm, k), lambda i: (i, 0)),
            pl.BlockSpec((k, d), lambda i: (0, 0)),
        ],
        out_specs=pl.BlockSpec((tm, d), lambda i: (i, 0)),
        out_shape=jax.ShapeDtypeStruct((m, d), F32),
        compiler_params=_cparams(("parallel",)),
        name="resid_mm",
    )(x, mm, w)


def _ple_kernel(x_ref, p_ref, h_ref, wp_ref, wg_ref, gn_ref, *o_refs, prompt_tiles):
    e = jnp.dot(p_ref[...], wp_ref[...], preferred_element_type=F32)
    gt = jnp.dot(h_ref[...], wg_ref[...], preferred_element_type=F32)
    x4 = x_ref[...] + e * _sigmoid(gt)
    if prompt_tiles is None:
        o_refs[0][...] = x4
    else:
        i = pl.program_id(0)
        y = _rms(x4, gn_ref[...])

        @pl.when(i < prompt_tiles)
        def _():
            o_refs[0][...] = y

        @pl.when(i >= prompt_tiles)
        def _():
            o_refs[1][...] = y


def _ple(x, p, h, wp, wg, g_final, *, m_prompt=None, tm=256):
    m, d = x.shape
    pd = p.shape[1]
    assert m % tm == 0
    if m_prompt is None:
        pt = None
        out_specs = [pl.BlockSpec((tm, d), lambda i: (i, 0))]
        out_shape = [jax.ShapeDtypeStruct((m, d), F32)]
    else:
        assert m_prompt % tm == 0
        pt = m_prompt // tm
        out_specs = [pl.BlockSpec((tm, d), lambda i: (jnp.minimum(i, pt - 1), 0)),
                     pl.BlockSpec((tm, d), lambda i: (jnp.maximum(i - pt, 0), 0))]
        out_shape = [jax.ShapeDtypeStruct((m_prompt, d), F32), jax.ShapeDtypeStruct((m - m_prompt, d), F32)]
    return pl.pallas_call(
        functools.partial(_ple_kernel, prompt_tiles=pt),
        grid=(m // tm,),
        in_specs=[
            pl.BlockSpec((tm, d), lambda i: (i, 0)),
            pl.BlockSpec((tm, pd), lambda i: (i, 0)),
            pl.BlockSpec((tm, d), lambda i: (i, 0)),
            pl.BlockSpec((pd, d), lambda i: (0, 0)),
            pl.BlockSpec((d, d), lambda i: (0, 0)),
            pl.BlockSpec((1, d), lambda i: (0, 0)),
        ],
        out_specs=out_specs,
        out_shape=out_shape,
        compiler_params=_cparams(("arbitrary",)),
        name="ple",
    )(x, p, h, wp, wg, g_final.reshape(1, d))


def _cumsum_kernel(lf_ref, o_ref, carry_sc, *, n_heads, tb, out_scale):
    @pl.when(pl.program_id(0) == 0)
    def _():
        carry_sc[...] = jnp.zeros_like(carry_sc)

    row = lax.broadcasted_iota(I32, (tb, tb), 0)
    col = lax.broadcasted_iota(I32, (tb, tb), 1)
    tri = jnp.where(col <= row, 1.0, 0.0).astype(F32)
    lf = lf_ref[...]
    for h in range(n_heads):
        lfb = jnp.broadcast_to(lf[:, h:h + 1], (tb, LANES))
        cum = jnp.dot(tri, lfb, preferred_element_type=F32, precision=lax.Precision.HIGHEST)
        cum = cum + carry_sc[h:h + 1, :]
        o_ref[h] = cum * out_scale
        carry_sc[h:h + 1, :] = cum[tb - 1:tb, :]


def _cumsum_rep(lf, *, out_scale, tb=256):
    s, n_heads = lf.shape
    assert s % tb == 0
    return pl.pallas_call(
        functools.partial(_cumsum_kernel, n_heads=n_heads, tb=tb, out_scale=out_scale),
        grid=(s // tb,),
        in_specs=[pl.BlockSpec((tb, n_heads), lambda i: (i, 0))],
        out_specs=pl.BlockSpec((n_heads, tb, LANES), lambda i: (0, i, 0)),
        out_shape=jax.ShapeDtypeStruct((n_heads, s, LANES), F32),
        scratch_shapes=[pltpu.VMEM((n_heads, LANES), F32)],
        compiler_params=_cparams(("arbitrary",)),
        name="cumsum_logf",
    )(lf)


def _idx_prompt_kernel(ki_ref, qt_ref, w_ref, mask_ref, ks_ref, hi_ref, lo_ref, lo2_ref, *,
                       n_sel, tq, n_heads, cb):
    i = pl.program_id(0)
    n_blocks = pl.num_programs(0)
    big = cb * tq

    @pl.when(i == 0)
    def _():
        ks_ref[...] = jnp.full(ks_ref.shape, NEG_INF_KEY, I32)
        hi_ref[...] = jnp.full(hi_ref.shape, _key_hi(NEG_INF_KEY), I16)
        lo_ref[...] = jnp.full(lo_ref.shape, _key_lo(NEG_INF_KEY), I16)

    w = w_ref[0]
    row = lax.broadcasted_iota(I32, (tq, tq), 0)
    col = lax.broadcasted_iota(I32, (tq, tq), 1)
    hpd = 2

    def score_chunk(c):
        r0 = pl.multiple_of(c * tq, tq)
        kc = ki_ref[pl.ds(r0, tq), :]
        acc = jnp.zeros((tq, tq), F32)
        for g in range(n_heads // hpd):
            sc = jnp.dot(kc, qt_ref[0, :, g * hpd * tq:(g + 1) * hpd * tq], preferred_element_type=F32)
            for u in range(hpd):
                h = g * hpd + u
                acc = acc + jnp.maximum(sc[:, u * tq:(u + 1) * tq], 0.0) * w[h:h + 1, :]
        acc = jnp.where(row + (c - i) * tq <= col, acc, NEG_INF)
        key = _sort_key(acc)
        ks_ref[pl.ds(r0, tq), :] = key
        hi_ref[pl.ds(r0, tq), :] = _key_hi(key).astype(I16)
        lo_ref[pl.ds(r0, tq), :] = _key_lo(key).astype(I16)

    def score_pair(pr, carry):
        score_chunk(2 * pr)
        score_chunk(2 * pr + 1)
        return carry

    lax.fori_loop(0, (i + 2) // 2, score_pair, 0)

    n_big = (i + cb) // cb

    def count_ge(cand):
        def body(c, acc):
            r0 = pl.multiple_of(c * big, big)
            hit = jnp.where(ks_ref[pl.ds(r0, big), :] >= cand, 1, 0).astype(I32)
            return acc + hit.reshape(big // SUBLANES, SUBLANES, tq).sum(axis=0)
        acc = lax.fori_loop(0, n_big, body, jnp.zeros((SUBLANES, tq), I32))
        return jnp.sum(acc, axis=0, keepdims=True)

    packed = 2 * SUBLANES

    def count16(ref, cand, strict):
        cand = cand.astype(I16)

        def body(c, acc):
            r0 = pl.multiple_of(c * big, big)
            blk = ref[pl.ds(r0, big), :]
            hit = jnp.where(blk > cand if strict else blk >= cand, jnp.bfloat16(1), jnp.bfloat16(0))
            parts = list(hit.reshape(big // packed, packed, tq))
            while len(parts) > 1:
                parts = [parts[u] + parts[u + 1] for u in range(0, len(parts), 2)]
            return acc + parts[0].astype(F32)

        acc = lax.fori_loop(0, n_big, body, jnp.zeros((packed, tq), F32))
        return jnp.sum(acc, axis=0, keepdims=True)

    def kth_half(ref, rank):
        def bisect(b, ans_u):
            cand_u = ans_u | jnp.left_shift(jnp.int32(1), 15 - b)
            cnt = count16(ref, cand_u - HALF_BIAS, False)
            return jnp.where(cnt >= rank, cand_u, ans_u)
        return lax.fori_loop(0, 16, bisect, jnp.zeros((1, tq), I32)) - HALF_BIAS

    thr_hi = kth_half(hi_ref, float(n_sel))
    rank_lo = n_sel - count16(hi_ref, thr_hi, True)
    thr_hi16 = thr_hi.astype(I16)

    def keep_lo(c, carry):
        r0 = pl.multiple_of(c * big, big)
        same = hi_ref[pl.ds(r0, big), :] == thr_hi16
        lo2_ref[pl.ds(r0, big), :] = jnp.where(same, lo_ref[pl.ds(r0, big), :], jnp.int16(-HALF_BIAS))
        return carry

    lax.fori_loop(0, n_big, keep_lo, 0)
    thr_lo = kth_half(lo2_ref, rank_lo)
    thr = thr_hi * (2 * HALF_BIAS) + (thr_lo + HALF_BIAS)
    tied = jnp.max(jnp.abs(count_ge(thr) - n_sel)) > 0

    @pl.when(jnp.logical_not(tied))
    def _():
        def finish(c, carry):
            r0 = pl.multiple_of(c * tq, tq)
            key = ks_ref[pl.ds(r0, tq), :]
            mask_ref[0, pl.ds(r0, tq), :] =jnp.where(key >= thr, 0.0, NEG_INF).astype(BF16)
            return carry

        lax.fori_loop(0, i + 1, finish, 0)

    @pl.when(tied)
    def _():
        need = (n_sel - count_ge(thr + 1)).astype(F32)
        strict_lower = jnp.where(col < row, 1.0, 0.0).astype(BF16)

        def finish(c, seen):
            r0 = pl.multiple_of(c * tq, tq)
            key = ks_ref[pl.ds(r0, tq), :]
            eq = jnp.where(key == thr, 1.0, 0.0)
            before = jnp.dot(strict_lower, eq.astype(BF16), preferred_element_type=F32) + seen
            tie_ok = jnp.where(before < need, eq, 0.0)
            sel = jnp.where(key > thr, 1.0, tie_ok)
            mask_ref[0, pl.ds(r0, tq), :] =jnp.where(sel > 0.0, 0.0, NEG_INF).astype(BF16)
            return seen + jnp.sum(eq, axis=0, keepdims=True)

        lax.fori_loop(0, i + 1, finish, jnp.zeros((1, tq), F32))

    def fill(c, carry):
        r0 = pl.multiple_of(c * tq, tq)
        mask_ref[0, pl.ds(r0, tq), :] =jnp.full((tq, tq), NEG_INF, BF16)
        return carry

    lax.fori_loop(i + 1, n_blocks, fill, 0)


def _idx_prompt(ki, qt, w, *, n_sel, tq=128, cb=8):
    s, idim = ki.shape
    nb, n_heads, _ = w.shape
    assert nb * tq == s and (s // tq) % cb == 0 and cb * tq >= n_sel
    return pl.pallas_call(
        functools.partial(_idx_prompt_kernel, n_sel=n_sel, tq=tq, n_heads=n_heads, cb=cb),
        grid=(nb,),
        in_specs=[
            pl.BlockSpec((s, idim), lambda i: (0, 0)),
            pl.BlockSpec((1, idim, n_heads * tq), lambda i: (i, 0, 0)),
            pl.BlockSpec((1, n_heads, tq), lambda i: (i, 0, 0)),
        ],
        out_specs=pl.BlockSpec((1, s, tq), lambda i: (i, 0, 0)),
        out_shape=jax.ShapeDtypeStruct((nb, s, tq), BF16),
        scratch_shapes=[pltpu.VMEM((s, tq), I32), pltpu.VMEM((s, tq), I16), pltpu.VMEM((s, tq), I16),
                        pltpu.VMEM((s, tq), I16)],
        compiler_params=_cparams(("arbitrary",)),
        name="idx_prompt",
    )(ki, qt, w)


def _attn_prompt_kernel(it_ref, jt_ref, qt_ref, k_ref, vt_ref, b1_ref, b2_ref, far_ref, o_ref,
                        m_sc, l_sc, acc_sc, s_sc, mx_sc, *, mode, t, n_heads, hd, scale):
    s = pl.program_id(0)
    i = it_ref[s]
    j = jt_ref[s]

    @pl.when(j == 0)
    def _():
        m_sc[...] = jnp.full(m_sc.shape, M_INIT, F32)
        l_sc[...] = jnp.zeros_like(l_sc)
        acc_sc[...] = jnp.zeros_like(acc_sc)

    def logits(near):
        if mode == "dsa":
            sel_bias = jnp.concatenate([b1_ref[u] for u in range(t // LANES)], axis=1).astype(F32)
        for h in range(n_heads):
            hs = slice(h * hd, (h + 1) * hd)
            st = jnp.dot(k_ref[:, hs], qt_ref[0, hs, :], preferred_element_type=F32) * (scale * LOG2E)
            if mode == "fox":
                st = st - jnp.concatenate([b1_ref[h]] * (t // LANES), axis=1)
                if near:
                    st = st + b2_ref[...]
            elif near:
                st = st + b2_ref[h, i - j] + sel_bias
            else:
                st = st + far_ref[h] + sel_bias
            s_sc[h] = st
            mx_sc[h] = jnp.max(st, axis=0, keepdims=True)

    n_near = 1 if mode == "fox" else 2
    pl.when(i - j < n_near)(functools.partial(logits, True))
    pl.when(i - j >= n_near)(functools.partial(logits, False))

    for h in range(n_heads):
        hs = slice(h * hd, (h + 1) * hd)
        st = s_sc[h]
        m_old = m_sc[h]
        m_new = jnp.maximum(m_old, mx_sc[h])
        alpha = jnp.exp2(m_old - m_new)
        p = jnp.exp2(st - m_new)
        l_sc[h] = alpha * l_sc[h] + jnp.sum(p, axis=0, keepdims=True)
        pv = jnp.dot(vt_ref[0, hs, :], p.astype(BF16), preferred_element_type=F32)
        acc_sc[hs, :] = alpha * acc_sc[hs, :] + pv
        m_sc[h] = m_new

    @pl.when(j == i)
    def _():
        for h in range(n_heads):
            hs = slice(h * hd, (h + 1) * hd)
            o_ref[0, hs, :] = (acc_sc[hs, :] / l_sc[h]).astype(BF16)


def _attn_prompt(mode, qt, k, vt, b1, b2, far, *, t, n_heads, hd):
    nb, width, _ = qt.shape
    s = nb * t
    assert t % LANES == 0
    pairs = [(i, j) for i in range(nb) for j in range(i + 1)]
    it = jnp.asarray([p[0] for p in pairs], I32)
    jt = jnp.asarray([p[1] for p in pairs], I32)
    if mode == "fox":
        b1_spec = pl.BlockSpec((n_heads, t, LANES), lambda g, it, jt: (0, jt[g], 0))
    else:
        b1_spec = pl.BlockSpec((t // LANES, t, LANES), lambda g, it, jt: (it[g], jt[g], 0))
    grid_spec = pltpu.PrefetchScalarGridSpec(
        num_scalar_prefetch=2,
        grid=(len(pairs),),
        in_specs=[
            pl.BlockSpec((1, width, t), lambda g, it, jt: (it[g], 0, 0)),
            pl.BlockSpec((t, width), lambda g, it, jt: (jt[g], 0)),
            pl.BlockSpec((1, width, t), lambda g, it, jt: (jt[g], 0, 0)),
            b1_spec,
            pl.BlockSpec(memory_space=pltpu.VMEM),
            pl.BlockSpec(memory_space=pltpu.SMEM),
        ],
        out_specs=pl.BlockSpec((1, width, t), lambda g, it, jt: (it[g], 0, 0)),
        scratch_shapes=[
            pltpu.VMEM((n_heads, 1, t), F32),
            pltpu.VMEM((n_heads, 1, t), F32),
            pltpu.VMEM((width, t), F32),
            pltpu.VMEM((n_heads, t, t), F32),
            pltpu.VMEM((n_heads, 1, t), F32),
        ],
    )
    return pl.pallas_call(
        functools.partial(_attn_prompt_kernel, mode=mode, t=t, n_heads=n_heads, hd=hd,
                          scale=hd ** -0.5),
        grid_spec=grid_spec,
        out_shape=jax.ShapeDtypeStruct((nb, width, t), BF16),
        compiler_params=_cparams(("arbitrary",)),
        name="attn_prompt_" + mode,
    )(it, jt, qt, k, vt, b1, b2, far)


def _idx_sample_kernel(pt_ref, *refs, n_sel, pps, page, n_steps, t_new, n_heads, group):
    page_refs = refs[:pps]
    qi_ref, w_ref, kn_ref, mask_ref, sc_sc = refs[pps:]
    st = pl.program_id(1)
    span = pps * page
    width = sc_sc.shape[1]
    n_past = n_steps * span
    qi = qi_ref[0]
    w = w_ref[0]

    def scores(kb):
        sc = lax.dot_general(qi, kb, (((1,), (1,)), ((), ())), preferred_element_type=F32)
        sc = jnp.maximum(sc, 0.0) * jnp.concatenate([w] * (kb.shape[0] // LANES), axis=1)
        acc = sc[0:SUBLANES, :]
        for h in range(1, n_heads):
            acc = acc + sc[h * SUBLANES:(h + 1) * SUBLANES, :]
        return acc

    slot = pl.program_id(0) % group
    r0 = pl.multiple_of(slot * SUBLANES, SUBLANES)
    kb = jnp.concatenate([r[0].astype(BF16) for r in page_refs], axis=0)
    c0 = pl.multiple_of(st * span, span)
    sc_sc[pl.ds(r0, SUBLANES), pl.ds(c0, span)] = _sort_key(scores(kb))

    @pl.when(st == n_steps - 1)
    def _():
        qrow = lax.broadcasted_iota(I32, (SUBLANES, page), 0)
        jcol = lax.broadcasted_iota(I32, (SUBLANES, page), 1)
        new = jnp.where((jcol <= qrow) & (jcol < t_new), scores(kn_ref[0]), NEG_INF)
        sc_sc[pl.ds(r0, SUBLANES), n_past:n_past + page] = _sort_key(new)

    @pl.when((st == n_steps - 1) & (slot == group - 1))
    def _():
        rows = group * SUBLANES

        def bisect(b, ans_u):
            cand_u = ans_u | jnp.left_shift(jnp.int32(1), 31 - b)
            cand = cand_u ^ jnp.int32(INT_MIN)
            cnt = jnp.sum(jnp.where(sc_sc[...] >= cand, 1, 0).astype(I32), axis=1, keepdims=True)
            return jnp.where(cnt >= n_sel, cand_u, ans_u)

        ans_u = lax.fori_loop(0, 32, bisect, jnp.zeros((rows, 1), I32))
        thr = ans_u ^ jnp.int32(INT_MIN)
        n_gt = jnp.sum(jnp.where(sc_sc[...] > thr, 1, 0).astype(I32), axis=1, keepdims=True)
        need = (n_sel - n_gt).astype(F32)

        r = lax.broadcasted_iota(I32, (LANES, LANES), 0)
        c = lax.broadcasted_iota(I32, (LANES, LANES), 1)
        strict_upper = jnp.where(r < c, 1.0, 0.0).astype(BF16)
        seen = jnp.zeros((rows, 1), F32)
        for blk in range(width // LANES):
            cols = slice(blk * LANES, (blk + 1) * LANES)
            kblk = sc_sc[:, cols]
            eq = jnp.where(kblk == thr, 1.0, 0.0)
            before = jnp.dot(eq.astype(BF16), strict_upper, preferred_element_type=F32) + seen
            tie_ok = jnp.where(before < need, eq, 0.0)
            sel = jnp.where(kblk > thr, 1.0, tie_ok)
            mask_ref[:, :, cols] = jnp.where(sel > 0.0, 0.0, NEG_INF).reshape(group, SUBLANES, LANES)
            seen = seen + jnp.sum(eq, axis=1, keepdims=True)


def _idx_sample(pt_flat, pool_ik, qi, w, kn, *, n_sel, n_pages, t_new, n_heads, pps=16, group=8):
    b = qi.shape[0]
    page, idim = pool_ik.shape[1:]
    pps = math.gcd(pps, n_pages)
    group = math.gcd(group, b)
    assert page == LANES
    n_steps = n_pages // pps
    width = n_pages * page + page

    def page_spec(r):
        return pl.BlockSpec((1, page, idim),
                            lambda bi, st, pt: (pt[bi * n_pages + st * pps + r], 0, 0))

    grid_spec = pltpu.PrefetchScalarGridSpec(
        num_scalar_prefetch=1,
        grid=(b, n_steps),
        in_specs=[page_spec(r) for r in range(pps)] + [
            pl.BlockSpec((1, n_heads * SUBLANES, idim), lambda bi, st, pt: (bi, 0, 0)),
            pl.BlockSpec((1, n_heads * SUBLANES, LANES), lambda bi, st, pt: (bi, 0, 0)),
            pl.BlockSpec((1, page, idim), lambda bi, st, pt: (bi, 0, 0)),
        ],
        out_specs=pl.BlockSpec((group, SUBLANES, width), lambda bi, st, pt: (bi // group, 0, 0)),
        scratch_shapes=[pltpu.VMEM((group * SUBLANES, width), I32)],
    )
    return pl.pallas_call(
        functools.partial(_idx_sample_kernel, n_sel=n_sel, pps=pps, page=page, n_steps=n_steps,
                          t_new=t_new, n_heads=n_heads, group=group),
        grid_spec=grid_spec,
        out_shape=jax.ShapeDtypeStruct((b, SUBLANES, width), F32),
        compiler_params=_cparams(("arbitrary", "arbitrary")),
        name="idx_sample",
    )(pt_flat, *([pool_ik] * pps), qi, w, kn)


def _attn_sample_kernel(pt_ref, *refs, mode, pps, page, n_steps, t_new, n_heads, hd, scale):
    k_refs = refs[:pps]
    v_refs = refs[pps:2 * pps]
    rest = refs[2 * pps:]
    if mode == "fox":
        lf_refs = rest[:pps]
        q_ref, kn_ref, vn_ref, lfn_ref, o_ref, m_sc, l_sc, acc_sc, carry_sc = rest[pps:]
    else:
        q_ref, kn_ref, vn_ref, mask_ref, t5_ref, o_ref, m_sc, l_sc, acc_sc = rest
    st = pl.program_id(1)
    rows = SUBLANES * n_heads
    n_past = n_steps * pps * page

    @pl.when(st == 0)
    def _():
        m_sc[...] = jnp.full(m_sc.shape, M_INIT, F32)
        l_sc[...] = jnp.zeros_like(l_sc)
        acc_sc[...] = jnp.zeros_like(acc_sc)
        if mode == "fox":
            carry_sc[...] = jnp.zeros_like(carry_sc)

    span = pps * page
    is_last = st == n_steps - 1

    if mode == "fox":
        ru = lax.broadcasted_iota(I32, (page, page), 0)
        cu = lax.broadcasted_iota(I32, (page, page), 1)
        lower_incl = jnp.where(cu <= ru, 1.0, 0.0).astype(BF16)
        blocks = []
        before = jnp.zeros((1, 3 * n_heads), F32)
        for r in list(lf_refs) + [lfn_ref]:
            c = jnp.dot(lower_incl, jnp.concatenate(_split3(r[0]), axis=1), preferred_element_type=F32)
            blocks.append(c + before)
            before = before + c[page - 1:page, :]
        cum_parts = jnp.concatenate(_split3(jnp.concatenate(blocks, axis=0)), axis=1)
        rh = lax.broadcasted_iota(I32, (rows, 9 * n_heads), 0) // SUBLANES
        ch = lax.broadcasted_iota(I32, (rows, 9 * n_heads), 1) % n_heads
        head_of_row = jnp.where(rh == ch, 1.0, 0.0).astype(BF16)
        cum_t = lax.dot_general(head_of_row, cum_parts, (((1,), (1,)), ((), ())),
                                preferred_element_type=F32)
        f_cum = cum_t + carry_sc[...]
        carry_sc[...] = f_cum[:, span - 1:span]
        qrow = lax.broadcasted_iota(I32, (rows, page), 0) % SUBLANES
        jcol = lax.broadcasted_iota(I32, (rows, page), 1)
        new_ok = (jcol <= qrow) & (jcol < t_new) & is_last
        bias = jnp.concatenate([-f_cum[:, :span], jnp.where(new_ok, -f_cum[:, span:], NEG_INF)], axis=1)
    else:
        def mask_rows(mk):
            return jnp.concatenate([mk] * n_heads, axis=0)

        c0 = pl.multiple_of(st * span, span)
        past_bias = t5_ref[:, pl.ds(c0, span)] + mask_rows(mask_ref[0, :, pl.ds(c0, span)])
        new_bias = t5_ref[:, n_past:n_past + page] + mask_rows(mask_ref[0, :, n_past:n_past + page])
        bias = jnp.concatenate([past_bias, jnp.where(is_last, new_bias, NEG_INF)], axis=1)

    def head_rows(page_refs, new_ref, h):
        return jnp.concatenate(
            [r[0, pl.ds(h, page, stride=n_heads), :].astype(BF16) for r in page_refs] + [new_ref[0, h]], axis=0)

    s = jnp.concatenate(
        [lax.dot_general(q_ref[0, h * SUBLANES:(h + 1) * SUBLANES, :], head_rows(k_refs, kn_ref, h),
                         (((1,), (1,)), ((), ())), preferred_element_type=F32) for h in range(n_heads)], axis=0)
    s = s * scale + bias
    m_old = m_sc[...]
    m_new = jnp.maximum(m_old, jnp.max(s, axis=1, keepdims=True))
    alpha = jnp.exp(m_old - m_new)
    p = jnp.exp(s - m_new)
    l_sc[...] = alpha * l_sc[...] + jnp.sum(p, axis=1, keepdims=True)
    p16 = p.astype(BF16)
    pv = jnp.concatenate(
        [jnp.dot(p16[h * SUBLANES:(h + 1) * SUBLANES, :], head_rows(v_refs, vn_ref, h),
                 preferred_element_type=F32) for h in range(n_heads)], axis=0)
    acc_sc[...] = alpha * acc_sc[...] + pv
    m_sc[...] = m_new

    @pl.when(is_last)
    def _():
        o_ref[0] = (acc_sc[...] / l_sc[...]).astype(BF16)


def _attn_sample(mode, pt_flat, pool_k, pool_v, q, kn, vn, extra, *, n_pages, t_new, n_heads, hd, pps=16):
    b, rows, _ = q.shape
    page = pool_k.shape[1] // n_heads
    pps = math.gcd(pps, n_pages)
    assert page == LANES and rows == SUBLANES * n_heads and t_new <= SUBLANES
    n_steps = n_pages // pps

    def page_spec(shape_tail, r):
        zeros = (0,) * len(shape_tail)
        return pl.BlockSpec((1,) + shape_tail,
                            lambda bi, st, pt: (pt[bi * n_pages + st * pps + r],) + zeros)

    def per_req(shape_tail):
        zeros = (0,) * len(shape_tail)
        return pl.BlockSpec((1,) + shape_tail, lambda bi, st, pt: (bi,) + zeros)

    in_specs = [page_spec((page * n_heads, hd), r) for r in range(pps)] * 2
    args = [pool_k] * pps + [pool_v] * pps
    scratch = [pltpu.VMEM((rows, 1), F32), pltpu.VMEM((rows, 1), F32), pltpu.VMEM((rows, hd), F32)]
    shared = [per_req((rows, hd)), per_req((n_heads, page, hd)), per_req((n_heads, page, hd))]
    if mode == "fox":
        pool_lf, lfn = extra
        in_specs += [page_spec((page, n_heads), r) for r in range(pps)]
        args += [pool_lf] * pps
        in_specs += shared + [per_req((page, n_heads))]
        args += [q, kn, vn, lfn]
        scratch.append(pltpu.VMEM((rows, 1), F32))
    else:
        mask, t5 = extra
        in_specs += shared + [per_req(mask.shape[1:]), pl.BlockSpec(memory_space=pltpu.VMEM)]
        args += [q, kn, vn, mask, t5]
    grid_spec = pltpu.PrefetchScalarGridSpec(
        num_scalar_prefetch=1,
        grid=(b, n_steps),
        in_specs=in_specs,
        out_specs=pl.BlockSpec((1, rows, hd), lambda bi, st, pt: (bi, 0, 0)),
        scratch_shapes=scratch,
    )
    return pl.pallas_call(
        functools.partial(_attn_sample_kernel, mode=mode, pps=pps, page=page, n_steps=n_steps,
                          t_new=t_new, n_heads=n_heads, hd=hd, scale=hd ** -0.5),
        grid_spec=grid_spec,
        out_shape=jax.ShapeDtypeStruct((b, rows, hd), BF16),
        compiler_params=_cparams(("arbitrary", "arbitrary")),
        name="attn_sample_" + mode,
    )(pt_flat, *args)


def _t5_bucket(dist, num_buckets):
    n = jnp.maximum(dist, 0)
    exact = num_buckets // 2
    nf = jnp.maximum(n, 1).astype(F32)
    large = exact + (jnp.log(nf / exact) / math.log(MAX_DISTANCE / exact)
                     * (num_buckets - exact)).astype(I32)
    return jnp.where(n < exact, n, jnp.minimum(large, num_buckets - 1))


def _t5_bias(rel_bias, dist):
    nb = rel_bias.shape[0]
    onehot = (_t5_bucket(dist, nb)[..., None] == jnp.arange(nb)).astype(F32)
    b = jnp.einsum("...n,nh->h...", onehot, rel_bias.astype(F32), precision=lax.Precision.HIGHEST)
    return jnp.where(dist[None] >= 0, b, NEG_INF)


def _is_pow2(x):
    return math.frexp(x)[0] == 0.5


def _mix_prompt(q_a, k_a, v_a, lf_a, q_b, k_b, v_b, q_i, k_i, w_i, rel_bias, *, n_heads, hd, t=512):
    s = q_a.shape[0]
    width = n_heads * hd
    to_blocks = lambda a: a.reshape(s // t, t, width).transpose(0, 2, 1)
    from_blocks = lambda a: a.transpose(0, 2, 1).reshape(s, width)
    idim = k_i.shape[1]
    ih = w_i.shape[1]
    n_sel = min(TOPK_MAX, s // 4)
    assert MAX_DISTANCE <= t + 1
    idx_scale, idx_w_scale = idim ** -0.5, ih ** -0.5
    assert _is_pow2(idx_scale) and _is_pow2(idx_w_scale)

    fk = _cumsum_rep(lf_a, out_scale=LOG2E)
    r = jnp.arange(t)
    dist0 = r[None, :] - r[:, None]
    causal = jnp.where(dist0 >= 0, 0.0, NEG_INF).astype(F32)
    o_a = from_blocks(_attn_prompt("fox", to_blocks(q_a), k_a, to_blocks(v_a), fk, causal,
                                   jnp.zeros((n_heads,), F32), t=t, n_heads=n_heads, hd=hd))

    tq = LANES
    nb = s // tq
    qt = q_i.reshape(nb, tq, ih, idim).transpose(0, 3, 2, 1).reshape(nb, idim, ih * tq)
    w = (w_i * (idx_scale * idx_w_scale)).reshape(nb, tq, ih).transpose(0, 2, 1)
    mask = _idx_prompt(k_i, qt, w, n_sel=n_sel, tq=tq)
    btiles = jnp.stack([_t5_bias(rel_bias, dist0 + nd * t) for nd in range(2)], axis=1) * LOG2E
    far = _t5_bias(rel_bias, jnp.full((1,), 2 * t, I32))[:, 0] * LOG2E
    o_b = from_blocks(_attn_prompt("dsa", to_blocks(q_b), k_b, to_blocks(v_b), mask, btiles, far,
                                   t=t, n_heads=n_heads, hd=hd))
    return o_a, o_b


def _head_major(x, n_heads, hd, t_pad):
    b, t_new, _ = x.shape
    xh = x.reshape(b, t_new, n_heads, hd).transpose(0, 2, 1, 3)
    return jnp.pad(xh, ((0, 0), (0, 0), (0, t_pad - t_new), (0, 0)))


def _pad_rows(x, rows):
    return jnp.pad(x, ((0, 0), (0, rows - x.shape[1]), (0, 0)))


def _mix_sample(q_a, k_a, v_a, lf_a, q_b, k_b, v_b, q_i, k_i, w_i, rel_bias, page_table,
                fox_k, fox_v, fox_lf, dsa_k, dsa_v, idx_k, *, layer, n_heads, hd):
    b, t_new, width = q_a.shape
    n_pages = page_table.shape[1]
    depth, n_pool, page = fox_k.shape[:3]
    idim = k_i.shape[2]
    ih = w_i.shape[2]
    past = n_pages * page
    n_sel = min(TOPK_MAX, (past + t_new) // 4)
    idx_scale, idx_w_scale = idim ** -0.5, ih ** -0.5
    assert _is_pow2(idx_scale) and _is_pow2(idx_w_scale)
    assert t_new <= SUBLANES
    pt_flat = page_table.reshape(-1).astype(I32) + layer * n_pool
    n_pool = depth * n_pool
    fox_lf = fox_lf.reshape(n_pool, page, n_heads)
    idx_k = idx_k.reshape(n_pool, page, idim)

    flat = lambda pool: pool.reshape(n_pool, page * n_heads, hd)
    q_rows = lambda q: _head_major(q, n_heads, hd, SUBLANES).reshape(b, n_heads * SUBLANES, hd)
    new_keys = lambda x: _head_major(x, n_heads, hd, page)
    o_a = _attn_sample("fox", pt_flat, flat(fox_k), flat(fox_v), q_rows(q_a), new_keys(k_a), new_keys(v_a),
                       (fox_lf, _pad_rows(lf_a, page)), n_pages=n_pages, t_new=t_new, n_heads=n_heads, hd=hd)

    qi = jnp.pad(q_i.reshape(b, t_new, ih, idim), ((0, 0), (0, SUBLANES - t_new), (0, 0), (0, 0)))
    qi = qi.transpose(0, 2, 1, 3).reshape(b, ih * SUBLANES, idim)
    wr = jnp.pad(w_i * (idx_scale * idx_w_scale), ((0, 0), (0, SUBLANES - t_new), (0, 0)))
    wr = jnp.broadcast_to(wr.transpose(0, 2, 1).reshape(b, ih * SUBLANES, 1), (b, ih * SUBLANES, LANES))
    mask = _idx_sample(pt_flat, idx_k, qi, wr, _pad_rows(k_i, page), n_sel=n_sel, n_pages=n_pages,
                       t_new=t_new, n_heads=ih)
    kpos = jnp.arange(past + page)
    qpos = past + jnp.arange(t_new)
    dist = jnp.where(kpos[None, :] < past + t_new, qpos[:, None] - kpos[None, :], -1)
    t5 = _t5_bias(rel_bias, dist)
    t5 = jnp.pad(t5, ((0, 0), (0, SUBLANES - t_new), (0, 0))).reshape(n_heads * SUBLANES, past + page)
    o_b = _attn_sample("dsa", pt_flat, flat(dsa_k), flat(dsa_v), q_rows(q_b), new_keys(k_b), new_keys(v_b),
                       (mask, t5), n_pages=n_pages, t_new=t_new, n_heads=n_heads, hd=hd)
    tokens = lambda o: o.reshape(b, n_heads, SUBLANES, hd)[:, :, :t_new].transpose(0, 2, 1, 3).reshape(b * t_new, width)
    return tokens(o_a), tokens(o_b)


def kernel(x_prompt, x_sample, cache_fox_k, cache_fox_v, cache_fox_logf, cache_dsa_k, cache_dsa_v, cache_idx_k, page_table, p_prompt, p_sample, rel_bias, g_ffn1, w1_pre, w3_pre, w2_pre, g_mix, w_in, b_forget, w_branch_fox, w_branch_dsa, w_out, g_ffn2, w1_post, w3_post, w2_post, g_ple, w_ple, w_ple_gate, g_final):
    depth = w_in.shape[0]
    bp, sp, d = x_prompt.shape
    bs, ts, _ = x_sample.shape
    assert bp == 1
    n_fox, hd = cache_fox_k.shape[3:]
    n_dsa = cache_dsa_k.shape[3]
    assert n_fox == n_dsa
    idim = cache_idx_k.shape[-1]
    ih = IDX_HEADS
    fw, dw = n_fox * hd, n_dsa * hd
    n_in = w_in.shape[2]
    assert n_in == 3 * fw + n_fox + 3 * dw + ih * idim + idim + ih + 2 * d
    mp, ms = bp * sp, bs * ts
    small_w = n_fox + idim + ih
    assert small_w <= LANES

    x = jnp.concatenate([x_prompt.reshape(mp, d), x_sample.reshape(ms, d)], axis=0)
    st_p, st_s = [], []
    for l in range(depth):
        offs = np.cumsum([0, fw, fw, fw, n_fox, dw, dw, dw, ih * idim, idim, ih, 2 * d])
        col = lambda a, c: w_in[l][:, offs[a]:offs[a] + c]
        w_q = jnp.concatenate([col(0, fw), col(4, dw), col(7, ih * idim)], axis=1).astype(BF16)
        w_kv = [col(c, fw).astype(BF16) for c in (1, 2, 5, 6)]
        w_gate = col(10, 2 * d).astype(BF16)
        w_small = jnp.concatenate([col(3, n_fox), col(8, idim), col(9, ih)], axis=1)
        w_small = jnp.pad(w_small, ((0, 0), (0, LANES - small_w))).astype(BF16)
        b_small = jnp.pad(b_forget[l], (0, LANES - n_fox)).reshape(1, LANES)

        x1, h_mix = _ffn(x, g_ffn1[l], w1_pre[l].astype(BF16), w3_pre[l].astype(BF16),
                         w2_pre[l].astype(BF16), g_mix[l])
        zq = _proj("q", h_mix, w_q)
        kv_p, kv_s, kv16 = zip(*[_proj_kv(h_mix, w, m_prompt=mp, n_heads=n_fox, hd=hd) for w in w_kv])
        gate = _proj("gate", h_mix, w_gate)
        small = _proj("small", h_mix, w_small, bias=b_small, n_fox=n_fox)

        q_a, q_b, q_i = zq[:, :fw], zq[:, fw:fw + dw], zq[:, fw + dw:]
        lf = small[:, :n_fox]
        k_i32 = small[:, n_fox:n_fox + idim]
        w_i = small[:, n_fox + idim:small_w]
        k_i16 = k_i32.astype(BF16)

        o_ap, o_bp = _mix_prompt(q_a[:mp], kv16[0], kv16[1][:mp], lf[:mp], q_b[:mp], kv16[2],
                                 kv16[3][:mp], q_i[:mp], k_i16[:mp], w_i[:mp], rel_bias,
                                 n_heads=n_fox, hd=hd)
        rs = lambda a: a[mp:].reshape(bs, ts, -1)
        o_as, o_bs = _mix_sample(rs(q_a), rs(kv16[0]), rs(kv16[1]), rs(lf), rs(q_b), rs(kv16[2]),
                                 rs(kv16[3]), rs(q_i), rs(k_i16), rs(w_i), rel_bias, page_table,
                                 cache_fox_k, cache_fox_v, cache_fox_logf, cache_dsa_k,
                                 cache_dsa_v, cache_idx_k, layer=l, n_heads=n_fox, hd=hd)
        o_a = jnp.concatenate([o_ap, o_as], axis=0)
        o_b = jnp.concatenate([o_bp, o_bs], axis=0)

        merged = _merge(o_a, o_b, gate, w_branch_fox[l].astype(BF16), w_branch_dsa[l].astype(BF16))
        x2 = _resid_mm(x1, merged, w_out[l].astype(BF16))
        x3, h_ple = _ffn(x2, g_ffn2[l], w1_post[l].astype(BF16), w3_post[l].astype(BF16),
                         w2_post[l].astype(BF16), g_ple[l])
        p = jnp.concatenate([p_prompt[l].reshape(mp, -1), p_sample[l].reshape(ms, -1)], axis=0).astype(BF16)
        ple_w = (w_ple[l].astype(BF16), w_ple_gate[l].astype(BF16), g_final)
        if l + 1 < depth:
            x, = _ple(x3, p, h_ple, *ple_w)
        else:
            y_p, y_s = _ple(x3, p, h_ple, *ple_w, m_prompt=mp)

        heads = lambda a, bb, tt: a.reshape(bb, tt, n_fox, hd)
        st_p.append((heads(kv_p[0], bp, sp), heads(kv_p[1], bp, sp), lf[:mp].reshape(bp, sp, n_fox),
                     heads(kv_p[2], bp, sp), heads(kv_p[3], bp, sp), k_i32[:mp].reshape(bp, sp, idim)))
        st_s.append((heads(kv_s[0], bs, ts), heads(kv_s[1], bs, ts), lf[mp:].reshape(bs, ts, n_fox),
                     heads(kv_s[2], bs, ts), heads(kv_s[3], bs, ts), k_i32[mp:].reshape(bs, ts, idim)))

    pfk, pfv, pflf, pdk, pdv, pik = [jnp.stack(a) for a in zip(*st_p)]
    sfk, sfv, sflf, sdk, sdv, sik = [jnp.stack(a) for a in zip(*st_s)]
    y_prompt = y_p.reshape(bp, sp, d)
    y_sample = y_s.reshape(bs, ts, d)
    return (y_prompt, y_sample, pfk, pfv, pflf, pdk, pdv, pik, sfk, sfv, sflf, sdk, sdv, sik)
```

```python
import functools
import math

import numpy as np
import jax
import jax.numpy as jnp
from jax import lax
from jax.experimental import pallas as pl
from jax.experimental.pallas import tpu as pltpu

F32 = jnp.float32
BF16 = jnp.bfloat16
I32 = jnp.int32

TOPK_MAX = 256
MAX_DISTANCE = 128
RMS_EPS = 1e-6
IDX_HEADS = 16

V7X_VMEM_BYTES = 64 * 1024 * 1024
VMEM_LIMIT = V7X_VMEM_BYTES - 8 * 1024 * 1024
LANES = 128
SUBLANES = 8

LOG2E = math.log2(math.e)
NEG_INF = float("-inf")
M_INIT = -1e30
INT_MIN = -2 ** 31
NEG_INF_KEY = int(np.int32(np.uint32(0xFF800000) ^ np.uint32(0x7FFFFFFF)))


def _cparams(sem):
    return pltpu.CompilerParams(dimension_semantics=sem, vmem_limit_bytes=VMEM_LIMIT)


def _rms(x, g):
    ms = jnp.mean(x * x, axis=-1, keepdims=True)
    return x * lax.rsqrt(ms + RMS_EPS) * g


def _sigmoid(x):
    return 1.0 / (1.0 + jnp.exp(-x))


def _split3(x):
    hi = x.astype(BF16)
    r1 = x - hi.astype(F32)
    mid = r1.astype(BF16)
    lo = (r1 - mid.astype(F32)).astype(BF16)
    return hi, mid, lo


def _sort_key(x):
    bits = pltpu.bitcast(x, I32)
    return bits ^ ((bits >> 31) & jnp.int32(0x7FFFFFFF))


def _ffn_kernel(x_ref, g_ref, w1_ref, w3_ref, w2_ref, gn_ref, o_ref, hn_ref, h_sc, acc_sc):
    f = pl.program_id(1)

    @pl.when(f == 0)
    def _():
        h_sc[...] = _rms(x_ref[...], g_ref[...]).astype(BF16)
        acc_sc[...] = jnp.zeros_like(acc_sc)

    h = h_sc[...]
    a = jnp.dot(h, w1_ref[...], preferred_element_type=F32)
    b = jnp.dot(h, w3_ref[...], preferred_element_type=F32)
    act = (a * _sigmoid(a) * b).astype(BF16)
    acc_sc[...] += jnp.dot(act, w2_ref[...], preferred_element_type=F32)

    @pl.when(f == pl.num_programs(1) - 1)
    def _():
        x1 = x_ref[...] + 0.5 * acc_sc[...]
        o_ref[...] = x1
        hn_ref[...] = _rms(x1, gn_ref[...]).astype(BF16)


def _ffn(x, g, w1, w3, w2, g_next, *, tm=512, tf=512):
    m, d = x.shape
    dff = w1.shape[1]
    assert m % tm == 0 and dff % tf == 0
    return pl.pallas_call(
        _ffn_kernel,
        grid=(m // tm, dff // tf),
        in_specs=[
            pl.BlockSpec((tm, d), lambda i, f: (i, 0)),
            pl.BlockSpec((1, d), lambda i, f: (0, 0)),
            pl.BlockSpec((d, tf), lambda i, f: (0, f)),
            pl.BlockSpec((d, tf), lambda i, f: (0, f)),
            pl.BlockSpec((tf, d), lambda i, f: (f, 0)),
            pl.BlockSpec((1, d), lambda i, f: (0, 0)),
        ],
        out_specs=[
            pl.BlockSpec((tm, d), lambda i, f: (i, 0)),
            pl.BlockSpec((tm, d), lambda i, f: (i, 0)),
        ],
        out_shape=[jax.ShapeDtypeStruct((m, d), F32), jax.ShapeDtypeStruct((m, d), BF16)],
        scratch_shapes=[pltpu.VMEM((tm, d), BF16), pltpu.VMEM((tm, d), F32)],
        compiler_params=_cparams(("parallel", "arbitrary")),
        name="ffn",
    )(x, g.reshape(1, d), w1, w3, w2, g_next.reshape(1, d))


def _proj_q_kernel(h_ref, w_ref, o16_ref):
    o16_ref[...] = jnp.dot(h_ref[...], w_ref[...], preferred_element_type=F32).astype(BF16)


def _proj_kv_kernel(h_ref, w_ref, op_ref, os_ref, o16_ref, *, n_heads, hd, prompt_tiles):
    i = pl.program_id(0)
    z = jnp.dot(h_ref[...], w_ref[...], preferred_element_type=F32)
    o16_ref[...] = z.astype(BF16)

    def put(dst):
        for h in range(n_heads):
            dst[:, h, :] = z[:, h * hd:(h + 1) * hd]

    pl.when(i < prompt_tiles)(functools.partial(put, op_ref))
    pl.when(i >= prompt_tiles)(functools.partial(put, os_ref))


def _proj_kv(h, w, *, m_prompt, n_heads, hd, tm=512):
    m, k = h.shape
    n = w.shape[1]
    m_sample = m - m_prompt
    assert n == n_heads * hd and m_prompt % tm == 0 and m_sample % tm == 0
    pt = m_prompt // tm
    return pl.pallas_call(
        functools.partial(_proj_kv_kernel, n_heads=n_heads, hd=hd, prompt_tiles=pt),
        grid=(m // tm,),
        in_specs=[pl.BlockSpec((tm, k), lambda i: (i, 0)), pl.BlockSpec((k, n), lambda i: (0, 0))],
        out_specs=[
            pl.BlockSpec((tm, n_heads, hd), lambda i: (jnp.minimum(i, pt - 1), 0, 0)),
            pl.BlockSpec((tm, n_heads, hd), lambda i: (jnp.maximum(i - pt, 0), 0, 0)),
            pl.BlockSpec((tm, n), lambda i: (i, 0)),
        ],
        out_shape=[jax.ShapeDtypeStruct((m_prompt, n_heads, hd), F32),
                   jax.ShapeDtypeStruct((m_sample, n_heads, hd), F32),
                   jax.ShapeDtypeStruct((m, n), BF16)],
        compiler_params=_cparams(("arbitrary",)),
        name="proj_kv",
    )(h, w)


def _proj_gate_kernel(h_ref, w_ref, o_ref):
    z = jnp.dot(h_ref[...], w_ref[...], preferred_element_type=F32)
    o_ref[...] = _sigmoid(z)


def _proj_small_kernel(h_ref, w_ref, b_ref, o_ref, *, n_fox):
    z = jnp.dot(h_ref[...], w_ref[...], preferred_element_type=F32)
    lane = lax.broadcasted_iota(I32, z.shape, 1)
    zf = z + b_ref[...]
    logsig = jnp.minimum(zf, 0.0) - jnp.log1p(jnp.exp(-jnp.abs(zf)))
    o_ref[...] = jnp.where(lane < n_fox, logsig, z)


def _proj(kind, h, w, *, tm=512, tn=1024, bias=None, n_fox=0):
    m, k = h.shape
    n = w.shape[1]
    tn = min(tn, n)
    assert m % tm == 0 and n % tn == 0
    grid = (n // tn, m // tm)
    h_spec = pl.BlockSpec((tm, k), lambda j, i: (i, 0))
    w_spec = pl.BlockSpec((k, tn), lambda j, i: (0, j))
    o_spec = pl.BlockSpec((tm, tn), lambda j, i: (i, j))
    cp = _cparams(("parallel", "parallel"))
    if kind == "q":
        return pl.pallas_call(
            _proj_q_kernel, grid=grid, in_specs=[h_spec, w_spec], out_specs=o_spec,
            out_shape=jax.ShapeDtypeStruct((m, n), BF16), compiler_params=cp, name="proj_q")(h, w)
    if kind == "gate":
        return pl.pallas_call(
            _proj_gate_kernel, grid=grid, in_specs=[h_spec, w_spec], out_specs=o_spec,
            out_shape=jax.ShapeDtypeStruct((m, n), F32), compiler_params=cp, name="proj_gate")(h, w)
    assert kind == "small"
    b_spec = pl.BlockSpec((1, tn), lambda j, i: (0, j))
    return pl.pallas_call(
        functools.partial(_proj_small_kernel, n_fox=n_fox), grid=grid,
        in_specs=[h_spec, w_spec, b_spec], out_specs=o_spec,
        out_shape=jax.ShapeDtypeStruct((m, n), F32), compiler_params=cp, name="proj_small")(h, w, bias)


def _merge_kernel(oa_ref, ob_ref, g_ref, wa_ref, wb_ref, m_ref, *, d):
    ua = jnp.dot(oa_ref[...], wa_ref[...], preferred_element_type=F32)
    ub = jnp.dot(ob_ref[...], wb_ref[...], preferred_element_type=F32)
    m_ref[...] = (g_ref[:, :d] * ua + g_ref[:, d:] * ub).astype(BF16)


def _merge(oa, ob, gate, wa, wb, *, tm=256):
    m, wdt = oa.shape
    d = wa.shape[1]
    assert m % tm == 0
    return pl.pallas_call(
        functools.partial(_merge_kernel, d=d),
        grid=(m // tm,),
        in_specs=[
            pl.BlockSpec((tm, wdt), lambda i: (i, 0)),
            pl.BlockSpec((tm, wdt), lambda i: (i, 0)),
            pl.BlockSpec((tm, 2 * d), lambda i: (i, 0)),
            pl.BlockSpec((wdt, d), lambda i: (0, 0)),
            pl.BlockSpec((wdt, d), lambda i: (0, 0)),
        ],
        out_specs=pl.BlockSpec((tm, d), lambda i: (i, 0)),
        out_shape=jax.ShapeDtypeStruct((m, d), BF16),
        compiler_params=_cparams(("parallel",)),
        name="merge",
    )(oa, ob, gate, wa, wb)


def _resid_mm_kernel(x_ref, m_ref, w_ref, o_ref):
    o_ref[...] = x_ref[...] + jnp.dot(m_ref[...], w_ref[...], preferred_element_type=F32)


def _resid_mm(x, mm, w, *, tm=256):
    m, d = x.shape
    k = mm.shape[1]
    assert m % tm == 0
    return pl.pallas_call(
        _resid_mm_kernel,
        grid=(m // tm,),
        in_specs=[
            pl.BlockSpec((tm, d), lambda i: (i, 0)),
            pl.BlockSpec((tm, k), lambda i: (i, 0)),
            pl.BlockSpec((k, d), lambda i: (0, 0)),
        ],
        out_specs=pl.BlockSpec((tm, d), lambda i: (i, 0)),
        out_shape=jax.ShapeDtypeStruct((m, d), F32),
        compiler_params=_cparams(("parallel",)),
        name="resid_mm",
    )(x, mm, w)


def _ple_kernel(x_ref, p_ref, h_ref, wp_ref, wg_ref, gn_ref, *o_refs, prompt_tiles):
    e = jnp.dot(p_ref[...], wp_ref[...], preferred_element_type=F32)
    gt = jnp.dot(h_ref[...], wg_ref[...], preferred_element_type=F32)
    x4 = x_ref[...] + e * _sigmoid(gt)
    if prompt_tiles is None:
        o_refs[0][...] = x4
    else:
        i = pl.program_id(0)
        y = _rms(x4, gn_ref[...])

        @pl.when(i < prompt_tiles)
        def _():
            o_refs[0][...] = y

        @pl.when(i >= prompt_tiles)
        def _():
            o_refs[1][...] = y


def _ple(x, p, h, wp, wg, g_final, *, m_prompt=None, tm=256):
    m, d = x.shape
    pd = p.shape[1]
    assert m % tm == 0
    if m_prompt is None:
        pt = None
        out_specs = [pl.BlockSpec((tm, d), lambda i: (i, 0))]
        out_shape = [jax.ShapeDtypeStruct((m, d), F32)]
    else:
        assert m_prompt % tm == 0
        pt = m_prompt // tm
        out_specs = [pl.BlockSpec((tm, d), lambda i: (jnp.minimum(i, pt - 1), 0)),
                     pl.BlockSpec((tm, d), lambda i: (jnp.maximum(i - pt, 0), 0))]
        out_shape = [jax.ShapeDtypeStruct((m_prompt, d), F32), jax.ShapeDtypeStruct((m - m_prompt, d), F32)]
    return pl.pallas_call(
        functools.partial(_ple_kernel, prompt_tiles=pt),
        grid=(m // tm,),
        in_specs=[
            pl.BlockSpec((tm, d), lambda i: (i, 0)),
            pl.BlockSpec((tm, pd), lambda i: (i, 0)),
            pl.BlockSpec((tm, d), lambda i: (i, 0)),
            pl.BlockSpec((pd, d), lambda i: (0, 0)),
            pl.BlockSpec((d, d), lambda i: (0, 0)),
            pl.BlockSpec((1, d), lambda i: (0, 0)),
        ],
        out_specs=out_specs,
        out_shape=out_shape,
        compiler_params=_cparams(("arbitrary",)),
        name="ple",
    )(x, p, h, wp, wg, g_final.reshape(1, d))


def _cumsum_kernel(lf_ref, o_ref, carry_sc, *, n_heads, tb, out_scale):
    @pl.when(pl.program_id(0) == 0)
    def _():
        carry_sc[...] = jnp.zeros_like(carry_sc)

    row = lax.broadcasted_iota(I32, (tb, tb), 0)
    col = lax.broadcasted_iota(I32, (tb, tb), 1)
    tri = jnp.where(col <= row, 1.0, 0.0).astype(F32)
    lf = lf_ref[...]
    for h in range(n_heads):
        lfb = jnp.broadcast_to(lf[:, h:h + 1], (tb, LANES))
        cum = jnp.dot(tri, lfb, preferred_element_type=F32, precision=lax.Precision.HIGHEST)
        cum = cum + carry_sc[h:h + 1, :]
        o_ref[h] = cum * out_scale
        carry_sc[h:h + 1, :] = cum[tb - 1:tb, :]


def _cumsum_rep(lf, *, out_scale, tb=256):
    s, n_heads = lf.shape
    assert s % tb == 0
    return pl.pallas_call(
        functools.partial(_cumsum_kernel, n_heads=n_heads, tb=tb, out_scale=out_scale),
        grid=(s // tb,),
        in_specs=[pl.BlockSpec((tb, n_heads), lambda i: (i, 0))],
        out_specs=pl.BlockSpec((n_heads, tb, LANES), lambda i: (0, i, 0)),
        out_shape=jax.ShapeDtypeStruct((n_heads, s, LANES), F32),
        scratch_shapes=[pltpu.VMEM((n_heads, LANES), F32)],
        compiler_params=_cparams(("arbitrary",)),
        name="cumsum_logf",
    )(lf)


def _idx_prompt_kernel(ki_ref, qt_ref, w_ref, mask_ref, ks_ref, *, n_sel, tq, n_heads, cb):
    i = pl.program_id(0)
    n_blocks = pl.num_programs(0)
    big = cb * tq

    @pl.when(i == 0)
    def _():
        ks_ref[...] = jnp.full(ks_ref.shape, NEG_INF_KEY, I32)

    w = w_ref[0]
    row = lax.broadcasted_iota(I32, (tq, tq), 0)
    col = lax.broadcasted_iota(I32, (tq, tq), 1)
    hpd = 2

    def score_chunk(c):
        r0 = pl.multiple_of(c * tq, tq)
        kc = ki_ref[pl.ds(r0, tq), :]
        acc = jnp.zeros((tq, tq), F32)
        for g in range(n_heads // hpd):
            sc = jnp.dot(kc, qt_ref[0, :, g * hpd * tq:(g + 1) * hpd * tq], preferred_element_type=F32)
            for u in range(hpd):
                h = g * hpd + u
                acc = acc + jnp.maximum(sc[:, u * tq:(u + 1) * tq], 0.0) * w[h:h + 1, :]
        acc = jnp.where(row + (c - i) * tq <= col, acc, NEG_INF)
        ks_ref[pl.ds(r0, tq), :] = _sort_key(acc)

    def score_group(gr, carry):
        for u in range(cb):
            score_chunk(cb * gr + u)
        return carry

    lax.fori_loop(0, (i + cb) // cb, score_group, 0)

    n_big = (i + cb) // cb

    def count_ge(cand):
        def body(c, acc):
            r0 = pl.multiple_of(c * big, big)
            hit = jnp.where(ks_ref[pl.ds(r0, big), :] >= cand, 1, 0).astype(I32)
            return acc + hit.reshape(big // SUBLANES, SUBLANES, tq).sum(axis=0)
        acc = lax.fori_loop(0, n_big, body, jnp.zeros((SUBLANES, tq), I32))
        return jnp.sum(acc, axis=0, keepdims=True)

    def bisect(b, ans_u):
        cand_u = ans_u | jnp.left_shift(jnp.int32(1), 31 - b)
        cnt = count_ge(cand_u ^ jnp.int32(INT_MIN))
        return jnp.where(cnt >= n_sel, cand_u, ans_u)

    ans_u = lax.fori_loop(0, 32, bisect, jnp.zeros((1, tq), I32))
    thr = ans_u ^ jnp.int32(INT_MIN)
    tied = jnp.max(jnp.abs(count_ge(thr) - n_sel)) > 0

    @pl.when(jnp.logical_not(tied))
    def _():
        def finish(c, carry):
            r0 = pl.multiple_of(c * tq, tq)
            key = ks_ref[pl.ds(r0, tq), :]
            mask_ref[0, pl.ds(r0, tq), :] =jnp.where(key >= thr, 0.0, NEG_INF).astype(BF16)
            return carry

        lax.fori_loop(0, i + 1, finish, 0)

    @pl.when(tied)
    def _():
        need = (n_sel - count_ge(thr + 1)).astype(F32)
        strict_lower = jnp.where(col < row, 1.0, 0.0).astype(BF16)

        def finish(c, seen):
            r0 = pl.multiple_of(c * tq, tq)
            key = ks_ref[pl.ds(r0, tq), :]
            eq = jnp.where(key == thr, 1.0, 0.0)
            before = jnp.dot(strict_lower, eq.astype(BF16), preferred_element_type=F32) + seen
            tie_ok = jnp.where(before < need, eq, 0.0)
            sel = jnp.where(key > thr, 1.0, tie_ok)
            mask_ref[0, pl.ds(r0, tq), :] =jnp.where(sel > 0.0, 0.0, NEG_INF).astype(BF16)
            return seen + jnp.sum(eq, axis=0, keepdims=True)

        lax.fori_loop(0, i + 1, finish, jnp.zeros((1, tq), F32))

    def fill(c, carry):
        r0 = pl.multiple_of(c * tq, tq)
        mask_ref[0, pl.ds(r0, tq), :] =jnp.full((tq, tq), NEG_INF, BF16)
        return carry

    lax.fori_loop(i + 1, n_blocks, fill, 0)


def _idx_prompt(ki, qt, w, *, n_sel, tq=128, cb=4):
    s, idim = ki.shape
    nb, n_heads, _ = w.shape
    assert nb * tq == s and (s // tq) % cb == 0 and cb * tq >= n_sel
    return pl.pallas_call(
        functools.partial(_idx_prompt_kernel, n_sel=n_sel, tq=tq, n_heads=n_heads, cb=cb),
        grid=(nb,),
        in_specs=[
            pl.BlockSpec((s, idim), lambda i: (0, 0)),
            pl.BlockSpec((1, idim, n_heads * tq), lambda i: (i, 0, 0)),
            pl.BlockSpec((1, n_heads, tq), lambda i: (i, 0, 0)),
        ],
        out_specs=pl.BlockSpec((1, s, tq), lambda i: (i, 0, 0)),
        out_shape=jax.ShapeDtypeStruct((nb, s, tq), BF16),
        scratch_shapes=[pltpu.VMEM((s, tq), I32)],
        compiler_params=_cparams(("arbitrary",)),
        name="idx_prompt",
    )(ki, qt, w)


def _attn_prompt_kernel(it_ref, jt_ref, qt_ref, k_ref, vt_ref, b1_ref, b2_ref, far_ref, o_ref,
                        m_sc, l_sc, acc_sc, s_sc, mx_sc, *, mode, t, n_heads, hd, scale):
    s = pl.program_id(0)
    i = it_ref[s]
    j = jt_ref[s]

    @pl.when(j == 0)
    def _():
        m_sc[...] = jnp.full(m_sc.shape, M_INIT, F32)
        l_sc[...] = jnp.zeros_like(l_sc)
        acc_sc[...] = jnp.zeros_like(acc_sc)

    def logits(near):
        if mode == "dsa":
            sel_bias = jnp.concatenate([b1_ref[u] for u in range(t // LANES)], axis=1).astype(F32)
        for h in range(n_heads):
            hs = slice(h * hd, (h + 1) * hd)
            st = jnp.dot(k_ref[:, hs], qt_ref[0, hs, :], preferred_element_type=F32) * (scale * LOG2E)
            if mode == "fox":
                st = st - jnp.concatenate([b1_ref[h]] * (t // LANES), axis=1)
                if near:
                    st = st + b2_ref[...]
            elif near:
                st = st + b2_ref[h, i - j] + sel_bias
            else:
                st = st + far_ref[h] + sel_bias
            s_sc[h] = st
            mx_sc[h] = jnp.max(st, axis=0, keepdims=True)

    n_near = 1 if mode == "fox" else 2
    pl.when(i - j < n_near)(functools.partial(logits, True))
    pl.when(i - j >= n_near)(functools.partial(logits, False))

    for h in range(n_heads):
        hs = slice(h * hd, (h + 1) * hd)
        st = s_sc[h]
        m_old = m_sc[h]
        m_new = jnp.maximum(m_old, mx_sc[h])
        alpha = jnp.exp2(m_old - m_new)
        p = jnp.exp2(st - m_new).astype(BF16)
        v_ones = jnp.concatenate([vt_ref[0, hs, :], jnp.ones((2 * SUBLANES, t), BF16)], axis=0)
        pv = jnp.dot(v_ones, p, preferred_element_type=F32)
        l_sc[h] = alpha * l_sc[h] + pv[hd:hd + 1, :]
        acc_sc[hs, :] = alpha * acc_sc[hs, :] + pv[:hd, :]
        m_sc[h] = m_new

    @pl.when(j == i)
    def _():
        for h in range(n_heads):
            hs = slice(h * hd, (h + 1) * hd)
            o_ref[0, hs, :] = (acc_sc[hs, :] / l_sc[h]).astype(BF16)


def _attn_prompt(mode, qt, k, vt, b1, b2, far, *, t, n_heads, hd):
    nb, width, _ = qt.shape
    s = nb * t
    assert t % LANES == 0
    pairs = [(i, j) for i in range(nb) for j in range(i + 1)]
    it = jnp.asarray([p[0] for p in pairs], I32)
    jt = jnp.asarray([p[1] for p in pairs], I32)
    if mode == "fox":
        b1_spec = pl.BlockSpec((n_heads, t, LANES), lambda g, it, jt: (0, jt[g], 0))
    else:
        b1_spec = pl.BlockSpec((t // LANES, t, LANES), lambda g, it, jt: (it[g], jt[g], 0))
    grid_spec = pltpu.PrefetchScalarGridSpec(
        num_scalar_prefetch=2,
        grid=(len(pairs),),
        in_specs=[
            pl.BlockSpec((1, width, t), lambda g, it, jt: (it[g], 0, 0)),
            pl.BlockSpec((t, width), lambda g, it, jt: (jt[g], 0)),
            pl.BlockSpec((1, width, t), lambda g, it, jt: (jt[g], 0, 0)),
            b1_spec,
            pl.BlockSpec(memory_space=pltpu.VMEM),
            pl.BlockSpec(memory_space=pltpu.SMEM),
        ],
        out_specs=pl.BlockSpec((1, width, t), lambda g, it, jt: (it[g], 0, 0)),
        scratch_shapes=[
            pltpu.VMEM((n_heads, 1, t), F32),
            pltpu.VMEM((n_heads, 1, t), F32),
            pltpu.VMEM((width, t), F32),
            pltpu.VMEM((n_heads, t, t), F32),
            pltpu.VMEM((n_heads, 1, t), F32),
        ],
    )
    return pl.pallas_call(
        functools.partial(_attn_prompt_kernel, mode=mode, t=t, n_heads=n_heads, hd=hd,
                          scale=hd ** -0.5),
        grid_spec=grid_spec,
        out_shape=jax.ShapeDtypeStruct((nb, width, t), BF16),
        compiler_params=_cparams(("arbitrary",)),
        name="attn_prompt_" + mode,
    )(it, jt, qt, k, vt, b1, b2, far)


def _idx_sample_kernel(pt_ref, *refs, n_sel, pps, page, n_steps, t_new, n_heads, group):
    page_refs = refs[:pps]
    qi_ref, w_ref, kn_ref, mask_ref, sc_sc = refs[pps:]
    st = pl.program_id(1)
    span = pps * page
    width = sc_sc.shape[1]
    n_past = n_steps * span
    qi = qi_ref[0]
    w = w_ref[0]

    def scores(kb):
        sc = lax.dot_general(qi, kb, (((1,), (1,)), ((), ())), preferred_element_type=F32)
        sc = jnp.maximum(sc, 0.0) * jnp.concatenate([w] * (kb.shape[0] // LANES), axis=1)
        acc = sc[0:SUBLANES, :]
        for h in range(1, n_heads):
            acc = acc + sc[h * SUBLANES:(h + 1) * SUBLANES, :]
        return acc

    slot = pl.program_id(0) % group
    r0 = pl.multiple_of(slot * SUBLANES, SUBLANES)
    kb = jnp.concatenate([r[0].astype(BF16) for r in page_refs], axis=0)
    c0 = pl.multiple_of(st * span, span)
    sc_sc[pl.ds(r0, SUBLANES), pl.ds(c0, span)] = _sort_key(scores(kb))

    @pl.when(st == n_steps - 1)
    def _():
        qrow = lax.broadcasted_iota(I32, (SUBLANES, page), 0)
        jcol = lax.broadcasted_iota(I32, (SUBLANES, page), 1)
        new = jnp.where((jcol <= qrow) & (jcol < t_new), scores(kn_ref[0]), NEG_INF)
        sc_sc[pl.ds(r0, SUBLANES), n_past:n_past + page] = _sort_key(new)

    @pl.when((st == n_steps - 1) & (slot == group - 1))
    def _():
        rows = group * SUBLANES

        def bisect(b, ans_u):
            cand_u = ans_u | jnp.left_shift(jnp.int32(1), 31 - b)
            cand = cand_u ^ jnp.int32(INT_MIN)
            cnt = jnp.sum(jnp.where(sc_sc[...] >= cand, 1, 0).astype(I32), axis=1, keepdims=True)
            return jnp.where(cnt >= n_sel, cand_u, ans_u)

        ans_u = lax.fori_loop(0, 32, bisect, jnp.zeros((rows, 1), I32))
        thr = ans_u ^ jnp.int32(INT_MIN)
        n_gt = jnp.sum(jnp.where(sc_sc[...] > thr, 1, 0).astype(I32), axis=1, keepdims=True)
        need = (n_sel - n_gt).astype(F32)

        r = lax.broadcasted_iota(I32, (LANES, LANES), 0)
        c = lax.broadcasted_iota(I32, (LANES, LANES), 1)
        strict_upper = jnp.where(r < c, 1.0, 0.0).astype(BF16)
        seen = jnp.zeros((rows, 1), F32)
        for blk in range(width // LANES):
            cols = slice(blk * LANES, (blk + 1) * LANES)
            kblk = sc_sc[:, cols]
            eq = jnp.where(kblk == thr, 1.0, 0.0)
            before = jnp.dot(eq.astype(BF16), strict_upper, preferred_element_type=F32) + seen
            tie_ok = jnp.where(before < need, eq, 0.0)
            sel = jnp.where(kblk > thr, 1.0, tie_ok)
            mask_ref[:, :, cols] = jnp.where(sel > 0.0, 0.0, NEG_INF).reshape(group, SUBLANES, LANES)
            seen = seen + jnp.sum(eq, axis=1, keepdims=True)


def _idx_sample(pt_flat, pool_ik, qi, w, kn, *, n_sel, n_pages, t_new, n_heads, pps=16, group=8):
    b = qi.shape[0]
    page, idim = pool_ik.shape[1:]
    pps = math.gcd(pps, n_pages)
    group = math.gcd(group, b)
    assert page == LANES
    n_steps = n_pages // pps
    width = n_pages * page + page

    def page_spec(r):
        return pl.BlockSpec((1, page, idim),
                            lambda bi, st, pt: (pt[bi * n_pages + st * pps + r], 0, 0))

    grid_spec = pltpu.PrefetchScalarGridSpec(
        num_scalar_prefetch=1,
        grid=(b, n_steps),
        in_specs=[page_spec(r) for r in range(pps)] + [
            pl.BlockSpec((1, n_heads * SUBLANES, idim), lambda bi, st, pt: (bi, 0, 0)),
            pl.BlockSpec((1, n_heads * SUBLANES, LANES), lambda bi, st, pt: (bi, 0, 0)),
            pl.BlockSpec((1, page, idim), lambda bi, st, pt: (bi, 0, 0)),
        ],
        out_specs=pl.BlockSpec((group, SUBLANES, width), lambda bi, st, pt: (bi // group, 0, 0)),
        scratch_shapes=[pltpu.VMEM((group * SUBLANES, width), I32)],
    )
    return pl.pallas_call(
        functools.partial(_idx_sample_kernel, n_sel=n_sel, pps=pps, page=page, n_steps=n_steps,
                          t_new=t_new, n_heads=n_heads, group=group),
        grid_spec=grid_spec,
        out_shape=jax.ShapeDtypeStruct((b, SUBLANES, width), F32),
        compiler_params=_cparams(("arbitrary", "arbitrary")),
        name="idx_sample",
    )(pt_flat, *([pool_ik] * pps), qi, w, kn)


def _attn_sample_kernel(pt_ref, *refs, mode, pps, page, n_steps, t_new, n_heads, hd, scale):
    k_refs = refs[:pps]
    v_refs = refs[pps:2 * pps]
    rest = refs[2 * pps:]
    if mode == "fox":
        lf_refs = rest[:pps]
        q_ref, kn_ref, vn_ref, lfn_ref, o_ref, m_sc, l_sc, acc_sc, carry_sc = rest[pps:]
    else:
        q_ref, kn_ref, vn_ref, mask_ref, t5_ref, o_ref, m_sc, l_sc, acc_sc = rest
    st = pl.program_id(1)
    rows = SUBLANES * n_heads
    n_past = n_steps * pps * page

    @pl.when(st == 0)
    def _():
        m_sc[...] = jnp.full(m_sc.shape, M_INIT, F32)
        l_sc[...] = jnp.zeros_like(l_sc)
        acc_sc[...] = jnp.zeros_like(acc_sc)
        if mode == "fox":
            carry_sc[...] = jnp.zeros_like(carry_sc)

    span = pps * page
    is_last = st == n_steps - 1

    if mode == "fox":
        ru = lax.broadcasted_iota(I32, (page, page), 0)
        cu = lax.broadcasted_iota(I32, (page, page), 1)
        lower_incl = jnp.where(cu <= ru, 1.0, 0.0).astype(BF16)
        blocks = []
        before = jnp.zeros((1, 3 * n_heads), F32)
        for r in list(lf_refs) + [lfn_ref]:
            c = jnp.dot(lower_incl, jnp.concatenate(_split3(r[0]), axis=1), preferred_element_type=F32)
            blocks.append(c + before)
            before = before + c[page - 1:page, :]
        cum_parts = jnp.concatenate(_split3(jnp.concatenate(blocks, axis=0)), axis=1)
        rh = lax.broadcasted_iota(I32, (rows, 9 * n_heads), 0) // SUBLANES
        ch = lax.broadcasted_iota(I32, (rows, 9 * n_heads), 1) % n_heads
        head_of_row = jnp.where(rh == ch, 1.0, 0.0).astype(BF16)
        cum_t = lax.dot_general(head_of_row, cum_parts, (((1,), (1,)), ((), ())),
                                preferred_element_type=F32)
        f_cum = cum_t + carry_sc[...]
        carry_sc[...] = f_cum[:, span - 1:span]
        qrow = lax.broadcasted_iota(I32, (rows, page), 0) % SUBLANES
        jcol = lax.broadcasted_iota(I32, (rows, page), 1)
        new_ok = (jcol <= qrow) & (jcol < t_new) & is_last
        bias = jnp.concatenate([-f_cum[:, :span], jnp.where(new_ok, -f_cum[:, span:], NEG_INF)], axis=1)
    else:
        def mask_rows(mk):
            return jnp.concatenate([mk] * n_heads, axis=0)

        c0 = pl.multiple_of(st * span, span)
        past_bias = t5_ref[:, pl.ds(c0, span)] + mask_rows(mask_ref[0, :, pl.ds(c0, span)])
        new_bias = t5_ref[:, n_past:n_past + page] + mask_rows(mask_ref[0, :, n_past:n_past + page])
        bias = jnp.concatenate([past_bias, jnp.where(is_last, new_bias, NEG_INF)], axis=1)

    def head_rows(page_refs, new_ref, h):
        return jnp.concatenate(
            [r[0, pl.ds(h, page, stride=n_heads), :].astype(BF16) for r in page_refs] + [new_ref[0, h]], axis=0)

    s = jnp.concatenate(
        [lax.dot_general(q_ref[0, h * SUBLANES:(h + 1) * SUBLANES, :], head_rows(k_refs, kn_ref, h),
                         (((1,), (1,)), ((), ())), preferred_element_type=F32) for h in range(n_heads)], axis=0)
    s = s * scale + bias
    m_old = m_sc[...]
    m_new = jnp.maximum(m_old, jnp.max(s, axis=1, keepdims=True))
    alpha = jnp.exp(m_old - m_new)
    p = jnp.exp(s - m_new)
    l_sc[...] = alpha * l_sc[...] + jnp.sum(p, axis=1, keepdims=True)
    p16 = p.astype(BF16)
    pv = jnp.concatenate(
        [jnp.dot(p16[h * SUBLANES:(h + 1) * SUBLANES, :], head_rows(v_refs, vn_ref, h),
                 preferred_element_type=F32) for h in range(n_heads)], axis=0)
    acc_sc[...] = alpha * acc_sc[...] + pv
    m_sc[...] = m_new

    @pl.when(is_last)
    def _():
        o_ref[0] = (acc_sc[...] / l_sc[...]).astype(BF16)


def _attn_sample(mode, pt_flat, pool_k, pool_v, q, kn, vn, extra, *, n_pages, t_new, n_heads, hd, pps=16):
    b, rows, _ = q.shape
    page = pool_k.shape[1] // n_heads
    pps = math.gcd(pps, n_pages)
    assert page == LANES and rows == SUBLANES * n_heads and t_new <= SUBLANES
    n_steps = n_pages // pps

    def page_spec(shape_tail, r):
        zeros = (0,) * len(shape_tail)
        return pl.BlockSpec((1,) + shape_tail,
                            lambda bi, st, pt: (pt[bi * n_pages + st * pps + r],) + zeros)

    def per_req(shape_tail):
        zeros = (0,) * len(shape_tail)
        return pl.BlockSpec((1,) + shape_tail, lambda bi, st, pt: (bi,) + zeros)

    in_specs = [page_spec((page * n_heads, hd), r) for r in range(pps)] * 2
    args = [pool_k] * pps + [pool_v] * pps
    scratch = [pltpu.VMEM((rows, 1), F32), pltpu.VMEM((rows, 1), F32), pltpu.VMEM((rows, hd), F32)]
    shared = [per_req((rows, hd)), per_req((n_heads, page, hd)), per_req((n_heads, page, hd))]
    if mode == "fox":
        pool_lf, lfn = extra
        in_specs += [page_spec((page, n_heads), r) for r in range(pps)]
        args += [pool_lf] * pps
        in_specs += shared + [per_req((page, n_heads))]
        args += [q, kn, vn, lfn]
        scratch.append(pltpu.VMEM((rows, 1), F32))
    else:
        mask, t5 = extra
        in_specs += shared + [per_req(mask.shape[1:]), pl.BlockSpec(memory_space=pltpu.VMEM)]
        args += [q, kn, vn, mask, t5]
    grid_spec = pltpu.PrefetchScalarGridSpec(
        num_scalar_prefetch=1,
        grid=(b, n_steps),
        in_specs=in_specs,
        out_specs=pl.BlockSpec((1, rows, hd), lambda bi, st, pt: (bi, 0, 0)),
        scratch_shapes=scratch,
    )
    return pl.pallas_call(
        functools.partial(_attn_sample_kernel, mode=mode, pps=pps, page=page, n_steps=n_steps,
                          t_new=t_new, n_heads=n_heads, hd=hd, scale=hd ** -0.5),
        grid_spec=grid_spec,
        out_shape=jax.ShapeDtypeStruct((b, rows, hd), BF16),
        compiler_params=_cparams(("arbitrary", "arbitrary")),
        name="attn_sample_" + mode,
    )(pt_flat, *args)


def _t5_bucket(dist, num_buckets):
    n = jnp.maximum(dist, 0)
    exact = num_buckets // 2
    nf = jnp.maximum(n, 1).astype(F32)
    large = exact + (jnp.log(nf / exact) / math.log(MAX_DISTANCE / exact)
                     * (num_buckets - exact)).astype(I32)
    return jnp.where(n < exact, n, jnp.minimum(large, num_buckets - 1))


def _t5_bias(rel_bias, dist):
    nb = rel_bias.shape[0]
    onehot = (_t5_bucket(dist, nb)[..., None] == jnp.arange(nb)).astype(F32)
    b = jnp.einsum("...n,nh->h...", onehot, rel_bias.astype(F32), precision=lax.Precision.HIGHEST)
    return jnp.where(dist[None] >= 0, b, NEG_INF)


def _is_pow2(x):
    return math.frexp(x)[0] == 0.5


def _mix_prompt(q_a, k_a, v_a, lf_a, q_b, k_b, v_b, q_i, k_i, w_i, rel_bias, *, n_heads, hd, t=512):
    s = q_a.shape[0]
    width = n_heads * hd
    to_blocks = lambda a: a.reshape(s // t, t, width).transpose(0, 2, 1)
    from_blocks = lambda a: a.transpose(0, 2, 1).reshape(s, width)
    idim = k_i.shape[1]
    ih = w_i.shape[1]
    n_sel = min(TOPK_MAX, s // 4)
    assert MAX_DISTANCE <= t + 1
    idx_scale, idx_w_scale = idim ** -0.5, ih ** -0.5
    assert _is_pow2(idx_scale) and _is_pow2(idx_w_scale)

    fk = _cumsum_rep(lf_a, out_scale=LOG2E)
    r = jnp.arange(t)
    dist0 = r[None, :] - r[:, None]
    causal = jnp.where(dist0 >= 0, 0.0, NEG_INF).astype(F32)
    o_a = from_blocks(_attn_prompt("fox", to_blocks(q_a), k_a, to_blocks(v_a), fk, causal,
                                   jnp.zeros((n_heads,), F32), t=t, n_heads=n_heads, hd=hd))

    tq = LANES
    nb = s // tq
    qt = q_i.reshape(nb, tq, ih, idim).transpose(0, 3, 2, 1).reshape(nb, idim, ih * tq)
    w = (w_i * (idx_scale * idx_w_scale)).reshape(nb, tq, ih).transpose(0, 2, 1)
    mask = _idx_prompt(k_i, qt, w, n_sel=n_sel, tq=tq)
    btiles = jnp.stack([_t5_bias(rel_bias, dist0 + nd * t) for nd in range(2)], axis=1) * LOG2E
    far = _t5_bias(rel_bias, jnp.full((1,), 2 * t, I32))[:, 0] * LOG2E
    o_b = from_blocks(_attn_prompt("dsa", to_blocks(q_b), k_b, to_blocks(v_b), mask, btiles, far,
                                   t=t, n_heads=n_heads, hd=hd))
    return o_a, o_b


def _head_major(x, n_heads, hd, t_pad):
    b, t_new, _ = x.shape
    xh = x.reshape(b, t_new, n_heads, hd).transpose(0, 2, 1, 3)
    return jnp.pad(xh, ((0, 0), (0, 0), (0, t_pad - t_new), (0, 0)))


def _pad_rows(x, rows):
    return jnp.pad(x, ((0, 0), (0, rows - x.shape[1]), (0, 0)))


def _mix_sample(q_a, k_a, v_a, lf_a, q_b, k_b, v_b, q_i, k_i, w_i, rel_bias, page_table,
                fox_k, fox_v, fox_lf, dsa_k, dsa_v, idx_k, *, layer, n_heads, hd):
    b, t_new, width = q_a.shape
    n_pages = page_table.shape[1]
    depth, n_pool, page = fox_k.shape[:3]
    idim = k_i.shape[2]
    ih = w_i.shape[2]
    past = n_pages * page
    n_sel = min(TOPK_MAX, (past + t_new) // 4)
    idx_scale, idx_w_scale = idim ** -0.5, ih ** -0.5
    assert _is_pow2(idx_scale) and _is_pow2(idx_w_scale)
    assert t_new <= SUBLANES
    pt_flat = page_table.reshape(-1).astype(I32) + layer * n_pool
    n_pool = depth * n_pool
    fox_lf = fox_lf.reshape(n_pool, page, n_heads)
    idx_k = idx_k.reshape(n_pool, page, idim)

    flat = lambda pool: pool.reshape(n_pool, page * n_heads, hd)
    q_rows = lambda q: _head_major(q, n_heads, hd, SUBLANES).reshape(b, n_heads * SUBLANES, hd)
    new_keys = lambda x: _head_major(x, n_heads, hd, page)
    o_a = _attn_sample("fox", pt_flat, flat(fox_k), flat(fox_v), q_rows(q_a), new_keys(k_a), new_keys(v_a),
                       (fox_lf, _pad_rows(lf_a, page)), n_pages=n_pages, t_new=t_new, n_heads=n_heads, hd=hd)

    qi = jnp.pad(q_i.reshape(b, t_new, ih, idim), ((0, 0), (0, SUBLANES - t_new), (0, 0), (0, 0)))
    qi = qi.transpose(0, 2, 1, 3).reshape(b, ih * SUBLANES, idim)
    wr = jnp.pad(w_i * (idx_scale * idx_w_scale), ((0, 0), (0, SUBLANES - t_new), (0, 0)))
    wr = jnp.broadcast_to(wr.transpose(0, 2, 1).reshape(b, ih * SUBLANES, 1), (b, ih * SUBLANES, LANES))
    mask = _idx_sample(pt_flat, idx_k, qi, wr, _pad_rows(k_i, page), n_sel=n_sel, n_pages=n_pages,
                       t_new=t_new, n_heads=ih)
    kpos = jnp.arange(past + page)
    qpos = past + jnp.arange(t_new)
    dist = jnp.where(kpos[None, :] < past + t_new, qpos[:, None] - kpos[None, :], -1)
    t5 = _t5_bias(rel_bias, dist)
    t5 = jnp.pad(t5, ((0, 0), (0, SUBLANES - t_new), (0, 0))).reshape(n_heads * SUBLANES, past + page)
    o_b = _attn_sample("dsa", pt_flat, flat(dsa_k), flat(dsa_v), q_rows(q_b), new_keys(k_b), new_keys(v_b),
                       (mask, t5), n_pages=n_pages, t_new=t_new, n_heads=n_heads, hd=hd)
    tokens = lambda o: o.reshape(b, n_heads, SUBLANES, hd)[:, :, :t_new].transpose(0, 2, 1, 3).reshape(b * t_new, width)
    return tokens(o_a), tokens(o_b)


def kernel(x_prompt, x_sample, cache_fox_k, cache_fox_v, cache_fox_logf, cache_dsa_k, cache_dsa_v, cache_idx_k, page_table, p_prompt, p_sample, rel_bias, g_ffn1, w1_pre, w3_pre, w2_pre, g_mix, w_in, b_forget, w_branch_fox, w_branch_dsa, w_out, g_ffn2, w1_post, w3_post, w2_post, g_ple, w_ple, w_ple_gate, g_final):
    depth = w_in.shape[0]
    bp, sp, d = x_prompt.shape
    bs, ts, _ = x_sample.shape
    assert bp == 1
    n_fox, hd = cache_fox_k.shape[3:]
    n_dsa = cache_dsa_k.shape[3]
    assert n_fox == n_dsa
    idim = cache_idx_k.shape[-1]
    ih = IDX_HEADS
    fw, dw = n_fox * hd, n_dsa * hd
    n_in = w_in.shape[2]
    assert n_in == 3 * fw + n_fox + 3 * dw + ih * idim + idim + ih + 2 * d
    mp, ms = bp * sp, bs * ts
    small_w = n_fox + idim + ih
    assert small_w <= LANES

    x = jnp.concatenate([x_prompt.reshape(mp, d), x_sample.reshape(ms, d)], axis=0)
    st_p, st_s = [], []
    for l in range(depth):
        offs = np.cumsum([0, fw, fw, fw, n_fox, dw, dw, dw, ih * idim, idim, ih, 2 * d])
        col = lambda a, c: w_in[l][:, offs[a]:offs[a] + c]
        w_q = jnp.concatenate([col(0, fw), col(4, dw), col(7, ih * idim)], axis=1).astype(BF16)
        w_kv = [col(c, fw).astype(BF16) for c in (1, 2, 5, 6)]
        w_gate = col(10, 2 * d).astype(BF16)
        w_small = jnp.concatenate([col(3, n_fox), col(8, idim), col(9, ih)], axis=1)
        w_small = jnp.pad(w_small, ((0, 0), (0, LANES - small_w))).astype(BF16)
        b_small = jnp.pad(b_forget[l], (0, LANES - n_fox)).reshape(1, LANES)

        x1, h_mix = _ffn(x, g_ffn1[l], w1_pre[l].astype(BF16), w3_pre[l].astype(BF16),
                         w2_pre[l].astype(BF16), g_mix[l])
        zq = _proj("q", h_mix, w_q)
        kv_p, kv_s, kv16 = zip(*[_proj_kv(h_mix, w, m_prompt=mp, n_heads=n_fox, hd=hd) for w in w_kv])
        gate = _proj("gate", h_mix, w_gate)
        small = _proj("small", h_mix, w_small, bias=b_small, n_fox=n_fox)

        q_a, q_b, q_i = zq[:, :fw], zq[:, fw:fw + dw], zq[:, fw + dw:]
        lf = small[:, :n_fox]
        k_i32 = small[:, n_fox:n_fox + idim]
        w_i = small[:, n_fox + idim:small_w]
        k_i16 = k_i32.astype(BF16)

        o_ap, o_bp = _mix_prompt(q_a[:mp], kv16[0], kv16[1][:mp], lf[:mp], q_b[:mp], kv16[2],
                                 kv16[3][:mp], q_i[:mp], k_i16[:mp], w_i[:mp], rel_bias,
                                 n_heads=n_fox, hd=hd)
        rs = lambda a: a[mp:].reshape(bs, ts, -1)
        o_as, o_bs = _mix_sample(rs(q_a), rs(kv16[0]), rs(kv16[1]), rs(lf), rs(q_b), rs(kv16[2]),
                                 rs(kv16[3]), rs(q_i), rs(k_i16), rs(w_i), rel_bias, page_table,
                                 cache_fox_k, cache_fox_v, cache_fox_logf, cache_dsa_k,
                                 cache_dsa_v, cache_idx_k, layer=l, n_heads=n_fox, hd=hd)
        o_a = jnp.concatenate([o_ap, o_as], axis=0)
        o_b = jnp.concatenate([o_bp, o_bs], axis=0)

        merged = _merge(o_a, o_b, gate, w_branch_fox[l].astype(BF16), w_branch_dsa[l].astype(BF16))
        x2 = _resid_mm(x1, merged, w_out[l].astype(BF16))
        x3, h_ple = _ffn(x2, g_ffn2[l], w1_post[l].astype(BF16), w3_post[l].astype(BF16),
                         w2_post[l].astype(BF16), g_ple[l])
        p = jnp.concatenate([p_prompt[l].reshape(mp, -1), p_sample[l].reshape(ms, -1)], axis=0).astype(BF16)
        ple_w = (w_ple[l].astype(BF16), w_ple_gate[l].astype(BF16), g_final)
        if l + 1 < depth:
            x, = _ple(x3, p, h_ple, *ple_w)
        else:
            y_p, y_s = _ple(x3, p, h_ple, *ple_w, m_prompt=mp)

        heads = lambda a, bb, tt: a.reshape(bb, tt, n_fox, hd)
        st_p.append((heads(kv_p[0], bp, sp), heads(kv_p[1], bp, sp), lf[:mp].reshape(bp, sp, n_fox),
                     heads(kv_p[2], bp, sp), heads(kv_p[3], bp, sp), k_i32[:mp].reshape(bp, sp, idim)))
        st_s.append((heads(kv_s[0], bs, ts), heads(kv_s[1], bs, ts), lf[mp:].reshape(bs, ts, n_fox),
                     heads(kv_s[2], bs, ts), heads(kv_s[3], bs, ts), k_i32[mp:].reshape(bs, ts, idim)))

    pfk, pfv, pflf, pdk, pdv, pik = [jnp.stack(a) for a in zip(*st_p)]
    sfk, sfv, sflf, sdk, sdv, sik = [jnp.stack(a) for a in zip(*st_s)]
    y_prompt = y_p.reshape(bp, sp, d)
    y_sample = y_s.reshape(bs, ts, d)
    return (y_prompt, y_sample, pfk, pfv, pflf, pdk, pdv, pik, sfk, sfv, sflf, sdk, sdv, sik)
```
